```python
import math
import jax, jax.numpy as jnp
from jax import lax
import numpy as np

D_MODEL = 1024
BATCH = 2
SEQ = 8192
DEPTH = 2

CHUNK = 64
N_LEFT_CHUNKS = 8
BAND_CHUNKS = N_LEFT_CHUNKS + 1
N_HEADS = 16
HEAD_DIM = D_MODEL // N_HEADS
MAX_REL = 2 * CHUNK
N_REL = 2 * MAX_REL + 1
CONV_WIDTH = 3
D_FF = ((8 * D_MODEL // 3 + 255) // 256) * 256
N_A = DEPTH // 2
N_B = DEPTH - N_A
EPS = 1e-6

kernel_name = "yoco_shortconv_chunkattn_sandwich_adaln"


def rms_norm(x, g):
    xf = x.astype(jnp.float32)
    y = xf * lax.rsqrt(jnp.mean(xf * xf, axis=-1, keepdims=True) + EPS)
    return (y * g.astype(jnp.float32)).astype(x.dtype)


def modulate(h, shift, scale):
    return h * (1.0 + scale[:, None, :]) + shift[:, None, :]


def short_conv_mixer(h, w_in, conv_k, w_out):
    S = h.shape[1]
    bcx = h @ w_in
    b_gate, c_gate, xin = jnp.split(bcx, 3, axis=-1)
    z = c_gate * xin
    zp = jnp.pad(z, ((0, 0), (CONV_WIDTH - 1, 0), (0, 0)))
    conv = sum(conv_k[k] * zp[:, k:k + S] for k in range(CONV_WIDTH))
    return (b_gate * conv) @ w_out


def gather_band(t):
    Bsz, S = t.shape[0], t.shape[1]
    nc = S // CHUNK
    tc = t.reshape(Bsz, nc, CHUNK, N_HEADS, HEAD_DIM)
    tp = jnp.pad(tc, ((0, 0), (N_LEFT_CHUNKS, 0), (0, 0), (0, 0), (0, 0)))
    idx = jnp.arange(nc)[:, None] + jnp.arange(BAND_CHUNKS)[None, :]
    band = tp[:, idx]
    return band.reshape(Bsz, nc, BAND_CHUNKS * CHUNK, N_HEADS, HEAD_DIM)


def chunk_band_attention(h, k_band, v_band, w_q, w_o, rel_bias):
    Bsz, S, _ = h.shape
    nc = S // CHUNK
    q = (h @ w_q).reshape(Bsz, nc, CHUNK, N_HEADS, HEAD_DIM)
    scores = jnp.einsum('bnqhd,bnkhd->bhnqk', q, k_band).astype(jnp.float32)
    scores = scores * (HEAD_DIM ** -0.5)
    a = jnp.arange(CHUNK)[:, None]
    kk = jnp.arange(BAND_CHUNKS * CHUNK)[None, :]
    j, b = kk // CHUNK, kk % CHUNK
    rel = (N_LEFT_CHUNKS - j) * CHUNK + a - b
    rel_idx = jnp.clip(rel, -MAX_REL, MAX_REL) + MAX_REL
    bias = rel_bias.astype(jnp.float32)[:, rel_idx]
    scores = scores + bias[None, :, None]
    key_chunk = jnp.arange(nc)[:, None] + (jnp.arange(BAND_CHUNKS * CHUNK)[None, :] // CHUNK) - N_LEFT_CHUNKS
    valid = key_chunk >= 0
    scores = jnp.where(valid[None, None, :, None, :], scores, jnp.finfo(jnp.float32).min)
    p = jax.nn.softmax(scores, axis=-1).astype(v_band.dtype)
    o = jnp.einsum('bhnqk,bnkhd->bnqhd', p, v_band)
    return o.reshape(Bsz, S, D_MODEL) @ w_o


def swiglu(h, w_in, w_out):
    gu = h @ w_in
    g, u = jnp.split(gu, 2, axis=-1)
    return (jax.nn.silu(g) * u) @ w_out


def setup_inputs(seed: int = 0) -> dict:
    key = jax.random.key(seed)
    ks = jax.random.split(key, 20)
    nrm = lambda k, shape, fan: jax.random.normal(k, shape, jnp.float32) * (fan ** -0.5)
    D = D_MODEL
    return {
        "x": jax.random.normal(ks[0], (BATCH, SEQ, D), jnp.float32),
        "c": jax.random.normal(ks[1], (BATCH, D), jnp.float32),
        "mod_w": nrm(ks[2], (DEPTH, D, 6 * D), D) * 0.3,
        "mod_b": 0.05 * jax.random.normal(ks[3], (DEPTH, 6 * D), jnp.float32),
        "norm_g": 1.0 + 0.05 * jax.random.normal(ks[4], (DEPTH, 4, D), jnp.float32),
        "ffn_w_in": nrm(ks[5], (DEPTH, D, 2 * D_FF), D),
        "ffn_w_out": nrm(ks[6], (DEPTH, D_FF, D), D_FF),
        "conv_w_in": nrm(ks[7], (N_A, D, 3 * D), D),
        "conv_k": nrm(ks[8], (N_A, CONV_WIDTH, D), CONV_WIDTH),
        "conv_w_out": nrm(ks[9], (N_A, D, D), D),
        "kv_mod_w": nrm(ks[10], (D, 2 * D), D) * 0.3,
        "kv_mod_b": 0.05 * jax.random.normal(ks[11], (2 * D,), jnp.float32),
        "kv_norm_g": 1.0 + 0.05 * jax.random.normal(ks[12], (D,), jnp.float32),
        "w_kv": nrm(ks[13], (D, 2 * D), D),
        "attn_w_q": nrm(ks[14], (N_B, D, D), D),
        "attn_w_o": nrm(ks[15], (N_B, D, D), D),
        "rel_bias": 0.5 * jax.random.normal(ks[16], (N_B, N_HEADS, N_REL), jnp.float32),
    }


def reference(x, c, mod_w, mod_b, norm_g, ffn_w_in, ffn_w_out, conv_w_in, conv_k,
              conv_w_out, kv_mod_w, kv_mod_b, kv_norm_g, w_kv, attn_w_q, attn_w_o,
              rel_bias):
    Bsz, S, _ = x.shape
    silu_c = jax.nn.silu(c)
    k_band = None
    v_band = None
    for layer in range(DEPTH):
        mod = silu_c @ mod_w[layer] + mod_b[layer]
        sh1, sc1, g1, sh2, sc2, g2 = jnp.split(mod, 6, axis=-1)
        h = modulate(rms_norm(x, norm_g[layer, 0]), sh1, sc1)
        if layer < N_A:
            y = short_conv_mixer(h, conv_w_in[layer], conv_k[layer], conv_w_out[layer])
        else:
            if layer == N_A:
                kv_sh, kv_sc = jnp.split(silu_c @ kv_mod_w + kv_mod_b, 2, axis=-1)
                hkv = modulate(rms_norm(x, kv_norm_g), kv_sh, kv_sc)
                k, v = jnp.split(hkv @ w_kv, 2, axis=-1)
                k_band = gather_band(k.reshape(Bsz, S, N_HEADS, HEAD_DIM))
                v_band = gather_band(v.reshape(Bsz, S, N_HEADS, HEAD_DIM))
            bi = layer - N_A
            y = chunk_band_attention(h, k_band, v_band, attn_w_q[bi], attn_w_o[bi], rel_bias[bi])
        x = x + g1[:, None, :] * rms_norm(y, norm_g[layer, 1])
        h = modulate(rms_norm(x, norm_g[layer, 2]), sh2, sc2)
        y = swiglu(h, ffn_w_in[layer], ffn_w_out[layer])
        x = x + g2[:, None, :] * rms_norm(y, norm_g[layer, 3])
    return x
```

```python
import functools

import jax
import jax.numpy as jnp
from jax import lax
from jax.experimental import pallas as pl
from jax.experimental.pallas import tpu as pltpu

EPS = 1e-6
CHUNK = 64
N_LEFT_CHUNKS = 8
BAND = (N_LEFT_CHUNKS + 1) * CHUNK
HEAD_DIM = 64
MAX_REL = 2 * CHUNK

V7X_LANES = 128
ROW_TILE = 512
V7X_VMEM_LIMIT = 56 * 1024 * 1024

BF16 = jnp.bfloat16
F32 = jnp.float32


def _dot(a, b):
    return jnp.dot(a, b, preferred_element_type=F32)


def _rms(x, g):
    return x * lax.rsqrt(jnp.mean(x * x, axis=-1, keepdims=True) + EPS) * g


def _resident(block_shape, index_map):
    return pl.BlockSpec(block_shape, index_map, pipeline_mode=pl.Buffered(1))


def _params(n_grid_dims):
    return pltpu.CompilerParams(
        dimension_semantics=("arbitrary",) * n_grid_dims,
        vmem_limit_bytes=V7X_VMEM_LIMIT)


def _modvec_kernel(c_ref, w_ref, b_ref, o_ref):
    s = jax.nn.silu(c_ref[...]).astype(BF16)
    o_ref[...] = _dot(s, w_ref[...].astype(BF16)) + b_ref[...]


def _modvec(c, w, b, col_tile=1024):
    n_layers, d, n = w.shape
    bsz = c.shape[0]
    return pl.pallas_call(
        _modvec_kernel,
        grid=(n_layers, n // col_tile),
        in_specs=[
            pl.BlockSpec((bsz, d), lambda l, j: (0, 0)),
            pl.BlockSpec((None, d, col_tile), lambda l, j: (l, 0, j)),
            pl.BlockSpec((None, 1, col_tile), lambda l, j: (l, 0, j)),
        ],
        out_specs=pl.BlockSpec((None, bsz, col_tile), lambda l, j: (l, 0, j)),
        out_shape=jax.ShapeDtypeStruct((n_layers, bsz, n), F32),
        compiler_params=_params(2),
        name="modvec",
    )(c, w, b)


def _mixer_kernel(x_ref, mod_ref, ng_ref, win_ref, ck_ref, wout_ref, o_ref, zbuf):
    tm, d = x_ref.shape
    x = x_ref[...]
    sh1, sc1, g1 = mod_ref[0:1, :], mod_ref[1:2, :], mod_ref[2:3, :]
    h = (_rms(x, ng_ref[0:1, :]) * (1.0 + sc1) + sh1).astype(BF16)
    b_gate = _dot(h, win_ref[:, 0:d])
    z = _dot(h, win_ref[:, d:2 * d]) * _dot(h, win_ref[:, 2 * d:3 * d])

    @pl.when(pl.program_id(1) == 0)
    def _():
        zbuf[0:8, :] = jnp.zeros((8, d), F32)

    zbuf[8:8 + tm, :] = z
    conv = (ck_ref[0:1, :] * zbuf[6:6 + tm, :]
            + ck_ref[1:2, :] * zbuf[7:7 + tm, :]
            + ck_ref[2:3, :] * z)
    zbuf[0:8, :] = zbuf[tm:tm + 8, :]
    y = _dot((b_gate * conv).astype(BF16), wout_ref[...])
    o_ref[...] = x + g1 * _rms(y, ng_ref[1:2, :])


def _mixer(x, mod, norm_g, layer, w_in, conv_k, w_out, a_idx):
    bsz, s, d = x.shape
    tm = ROW_TILE
    return pl.pallas_call(
        _mixer_kernel,
        grid=(bsz, s // tm),
        in_specs=[
            pl.BlockSpec((None, tm, d), lambda b, i: (b, i, 0)),
            pl.BlockSpec((None, None, 6, d), lambda b, i: (layer, b, 0, 0)),
            _resident((None, 4, d), lambda b, i: (layer, 0, 0)),
            _resident((None, d, 3 * d), lambda b, i: (a_idx, 0, 0)),
            _resident((None, 3, d), lambda b, i: (a_idx, 0, 0)),
            _resident((None, d, d), lambda b, i: (a_idx, 0, 0)),
        ],
        out_specs=pl.BlockSpec((None, tm, d), lambda b, i: (b, i, 0)),
        out_shape=jax.ShapeDtypeStruct(x.shape, F32),
        scratch_shapes=[pltpu.VMEM((tm + 8, d), F32)],
        compiler_params=_params(2),
        name="conv_mixer",
    )(x, mod, norm_g, w_in, conv_k, w_out)


def _ffn_kernel(x_ref, mod_ref, ng_ref, win_ref, wout_ref, o_ref, *, n_split):
    x = x_ref[...]
    sh2, sc2, g2 = mod_ref[3:4, :], mod_ref[4:5, :], mod_ref[5:6, :]
    h = (_rms(x, ng_ref[2:3, :]) * (1.0 + sc2) + sh2).astype(BF16)
    f = wout_ref.shape[0]
    fc = f // n_split
    y = None
    for j in range(n_split):
        g = _dot(h, win_ref[:, j * fc:(j + 1) * fc])
        u = _dot(h, win_ref[:, f + j * fc:f + (j + 1) * fc])
        yj = _dot((jax.nn.silu(g) * u).astype(BF16), wout_ref[j * fc:(j + 1) * fc, :])
        y = yj if y is None else y + yj
    o_ref[...] = x + g2 * _rms(y, ng_ref[3:4, :])


def _ffn(x, mod, norm_g, layer, w_in, w_out):
    bsz, s, d = x.shape
    f = w_out.shape[1]
    tm = ROW_TILE
    return pl.pallas_call(
        functools.partial(_ffn_kernel, n_split=2),
        grid=(bsz, s // tm),
        in_specs=[
            pl.BlockSpec((None, tm, d), lambda b, i: (b, i, 0)),
            pl.BlockSpec((None, None, 6, d), lambda b, i: (layer, b, 0, 0)),
            _resident((None, 4, d), lambda b, i: (layer, 0, 0)),
            _resident((None, d, 2 * f), lambda b, i: (layer, 0, 0)),
            _resident((None, f, d), lambda b, i: (layer, 0, 0)),
        ],
        out_specs=pl.BlockSpec((None, tm, d), lambda b, i: (b, i, 0)),
        out_shape=jax.ShapeDtypeStruct(x.shape, F32),
        compiler_params=_params(2),
        name="swiglu_ffn",
    )(x, mod, norm_g, w_in, w_out)


def _proj_kernel(x_ref, mod_ref, kvmod_ref, ng_ref, kvg_ref, wkv_ref, wq_ref,
                 k_ref, v_ref, q_ref):
    d = x_ref.shape[1]
    x = x_ref[...]
    xn = x * lax.rsqrt(jnp.mean(x * x, axis=-1, keepdims=True) + EPS)
    hkv = ((xn * kvg_ref[...]) * (1.0 + kvmod_ref[1:2, :]) + kvmod_ref[0:1, :]).astype(BF16)
    k_ref[...] = _dot(hkv, wkv_ref[:, 0:d]).astype(BF16)
    v_ref[...] = _dot(hkv, wkv_ref[:, d:2 * d]).astype(BF16)
    h = ((xn * ng_ref[0:1, :]) * (1.0 + mod_ref[1:2, :]) + mod_ref[0:1, :]).astype(BF16)
    q_ref[...] = (_dot(h, wq_ref[...]) * (HEAD_DIM ** -0.5)).astype(BF16)


def _proj(x, mod, kvmod, norm_g, layer, kv_norm_g, w_kv, w_q, b_idx):
    bsz, s, d = x.shape
    tm = ROW_TILE
    row_spec = pl.BlockSpec((None, tm, d), lambda b, i: (b, i, 0))
    out = jax.ShapeDtypeStruct(x.shape, BF16)
    return pl.pallas_call(
        _proj_kernel,
        grid=(bsz, s // tm),
        in_specs=[
            row_spec,
            pl.BlockSpec((None, None, 6, d), lambda b, i: (layer, b, 0, 0)),
            pl.BlockSpec((None, None, 2, d), lambda b, i: (0, b, 0, 0)),
            _resident((None, 4, d), lambda b, i: (layer, 0, 0)),
            _resident((1, d), lambda b, i: (0, 0)),
            _resident((d, 2 * d), lambda b, i: (0, 0)),
            _resident((None, d, d), lambda b, i: (b_idx, 0, 0)),
        ],
        out_specs=[row_spec, row_spec, row_spec],
        out_shape=[out, out, out],
        compiler_params=_params(2),
        name="qkv_proj",
    )(x, mod, kvmod, norm_g, kv_norm_g, w_kv, w_q)


def _bias_kernel(rb_ref, o_ref):
    n_heads = rb_ref.shape[0]
    n_tab = 2 * MAX_REL
    rb = rb_ref[:, 0:n_tab]
    hi = rb.astype(BF16)
    r1 = rb - hi.astype(F32)
    mid = r1.astype(BF16)
    lo = (r1 - mid.astype(F32)).astype(BF16)
    rb_top = rb_ref[:, n_tab:n_tab + 1]
    row = lax.broadcasted_iota(jnp.int32, (n_tab, BAND), 0)
    kk = lax.broadcasted_iota(jnp.int32, (n_tab, BAND), 1)
    base = (N_LEFT_CHUNKS - lax.shift_right_logical(kk, 6)) * CHUNK - (kk & (CHUNK - 1))

    def body(a, carry):
        idx = jnp.clip(base + a, -MAX_REL, MAX_REL) + MAX_REL
        onehot = jnp.where(row == idx, 1.0, 0.0).astype(BF16)
        t = _dot(hi, onehot) + _dot(mid, onehot) + _dot(lo, onehot)
        t = t + jnp.where(idx[0:1, :] == n_tab, rb_top, 0.0)
        for h in range(n_heads):
            o_ref[h, pl.ds(a, 1), :] = t[h:h + 1, :]
        return carry

    lax.fori_loop(0, CHUNK, body, 0)


def _bias_table(rel_bias, b_idx):
    n_heads, n_rel = rel_bias.shape[1:]
    return pl.pallas_call(
        _bias_kernel,
        grid=(1,),
        in_specs=[pl.BlockSpec((None, n_heads, n_rel), lambda i: (b_idx, 0, 0))],
        out_specs=pl.BlockSpec((n_heads, CHUNK, BAND), lambda i: (0, 0, 0)),
        out_shape=jax.ShapeDtypeStruct((n_heads, CHUNK, BAND), F32),
        compiler_params=_params(1),
        name="rel_bias_table",
    )(rel_bias)


def _attn_kernel(x_ref, q_ref, kp_ref, kc_ref, vp_ref, vc_ref, bias_ref, wo_ref,
                 mod_ref, ng_ref, o_ref, kbuf, vbuf, obuf):
    tm, d = q_ref.shape
    kbuf[0:tm, :] = kp_ref[...]
    kbuf[tm:2 * tm, :] = kc_ref[...]
    vbuf[0:tm, :] = vp_ref[...]
    vbuf[tm:2 * tm, :] = vc_ref[...]
    first_tile = pl.program_id(1) == 0
    low_head = lax.broadcasted_iota(jnp.int32, (CHUNK, V7X_LANES), 1) < HEAD_DIM
    col = lax.broadcasted_iota(jnp.int32, (2 * CHUNK, BAND), 1)
    neg = jnp.finfo(F32).min
    for c in range(tm // CHUNK):
        r0 = c * CHUNK
        valid = col >= jnp.where(first_tile, tm - r0, 0)
        for p in range(d // V7X_LANES):
            ls = slice(p * V7X_LANES, (p + 1) * V7X_LANES)
            qp = q_ref[r0:r0 + CHUNK, ls]
            zero = jnp.zeros_like(qp)
            q2 = jnp.concatenate([jnp.where(low_head, qp, zero),
                                  jnp.where(low_head, zero, qp)], axis=0)
            s = lax.dot_general(q2, kbuf[r0:r0 + BAND, ls], (((1,), (1,)), ((), ())),
                                preferred_element_type=F32)
            s = jnp.where(valid, s + bias_ref[p], neg)
            e = jnp.exp(s - jnp.max(s, axis=-1, keepdims=True))
            l = jnp.sum(e, axis=-1, keepdims=True)
            o2 = _dot(e.astype(BF16), vbuf[r0:r0 + BAND, ls]) / l
            o = jnp.where(low_head, o2[0:CHUNK, :], o2[CHUNK:2 * CHUNK, :])
            obuf[r0:r0 + CHUNK, ls] = o.astype(BF16)
    y = _dot(obuf[...], wo_ref[...])
    o_ref[...] = x_ref[...] + mod_ref[2:3, :] * _rms(y, ng_ref[1:2, :])


def _attn(x, q, k, v, bias, w_o, mod, norm_g, layer, b_idx):
    bsz, s, d = x.shape
    tm = ROW_TILE
    assert tm == N_LEFT_CHUNKS * CHUNK
    row_spec = pl.BlockSpec((None, tm, d), lambda b, i: (b, i, 0))
    prev_spec = pl.BlockSpec((None, tm, d), lambda b, i: (b, jnp.maximum(i - 1, 0), 0))
    n_pairs = d // V7X_LANES
    return pl.pallas_call(
        _attn_kernel,
        grid=(bsz, s // tm),
        in_specs=[
            row_spec, row_spec, prev_spec, row_spec, prev_spec, row_spec,
            _resident((n_pairs, 2 * CHUNK, BAND), lambda b, i: (0, 0, 0)),
            _resident((None, d, d), lambda b, i: (b_idx, 0, 0)),
            pl.BlockSpec((None, None, 6, d), lambda b, i: (layer, b, 0, 0)),
            _resident((None, 4, d), lambda b, i: (layer, 0, 0)),
        ],
        out_specs=row_spec,
        out_shape=jax.ShapeDtypeStruct(x.shape, F32),
        scratch_shapes=[pltpu.VMEM((2 * tm, d), BF16), pltpu.VMEM((2 * tm, d), BF16),
                        pltpu.VMEM((tm, d), BF16)],
        compiler_params=_params(2),
        name="band_attention",
    )(x, q, k, k, v, v, bias, w_o, mod, norm_g)


def kernel(x, c, mod_w, mod_b, norm_g, ffn_w_in, ffn_w_out, conv_w_in, conv_k, conv_w_out,
           kv_mod_w, kv_mod_b, kv_norm_g, w_kv, attn_w_q, attn_w_o, rel_bias):
    bsz, s, d = x.shape
    depth = mod_w.shape[0]
    n_a = conv_w_in.shape[0]
    n_heads = rel_bias.shape[1]
    assert d // n_heads == HEAD_DIM and s % ROW_TILE == 0

    mod = _modvec(c, mod_w, mod_b.reshape(depth, 1, 6 * d)).reshape(depth, bsz, 6, d)
    kvmod = _modvec(c, kv_mod_w[None], kv_mod_b.reshape(1, 1, 2 * d)).reshape(1, bsz, 2, d)

    ffn_w_in, ffn_w_out = ffn_w_in.astype(BF16), ffn_w_out.astype(BF16)
    conv_w_in, conv_w_out = conv_w_in.astype(BF16), conv_w_out.astype(BF16)
    w_kv, attn_w_q, attn_w_o = w_kv.astype(BF16), attn_w_q.astype(BF16), attn_w_o.astype(BF16)
    kv_norm_g = kv_norm_g.reshape(1, d)

    k = v = None
    for layer in range(depth):
        if layer < n_a:
            x = _mixer(x, mod, norm_g, layer, conv_w_in, conv_k, conv_w_out, layer)
        else:
            bi = layer - n_a
            k_new, v_new, q = _proj(x, mod, kvmod, norm_g, layer, kv_norm_g, w_kv, attn_w_q, bi)
            if layer == n_a:
                k, v = k_new, v_new
            bias = _bias_table(rel_bias, bi).reshape(n_heads // 2, 2 * CHUNK, BAND)
            x = _attn(x, q, k, v, bias, attn_w_o, mod, norm_g, layer, bi)
        x = _ffn(x, mod, norm_g, layer, ffn_w_in, ffn_w_out)
    return x
```

```python
import functools

import jax
import jax.numpy as jnp
from jax import lax
from jax.experimental import pallas as pl
from jax.experimental.pallas import tpu as pltpu

EPS = 1e-6
CHUNK = 64
N_LEFT_CHUNKS = 8
BAND = (N_LEFT_CHUNKS + 1) * CHUNK
SLAB = BAND + CHUNK
GROUP = 4
WIN = (GROUP + N_LEFT_CHUNKS) * CHUNK
HEAD_DIM = 64
MAX_REL = 2 * CHUNK

V7X_LANES = 128
ROW_TILE = 512
V7X_VMEM_LIMIT = 56 * 1024 * 1024

BF16 = jnp.bfloat16
F32 = jnp.float32


def _dot(a, b):
    return jnp.dot(a, b, preferred_element_type=F32)


def _rms(x, g):
    return x * lax.rsqrt(jnp.mean(x * x, axis=-1, keepdims=True) + EPS) * g


def _resident(block_shape, index_map):
    return pl.BlockSpec(block_shape, index_map, pipeline_mode=pl.Buffered(1))


def _params(n_grid_dims):
    return pltpu.CompilerParams(
        dimension_semantics=("arbitrary",) * n_grid_dims,
        vmem_limit_bytes=V7X_VMEM_LIMIT)


def _modvec_kernel(c_ref, w_ref, b_ref, o_ref):
    s = jax.nn.silu(c_ref[...]).astype(BF16)
    o_ref[...] = _dot(s, w_ref[...].astype(BF16)) + b_ref[...]


def _modvec(c, w, b, col_tile=1024):
    n_layers, d, n = w.shape
    bsz = c.shape[0]
    return pl.pallas_call(
        _modvec_kernel,
        grid=(n_layers, n // col_tile),
        in_specs=[
            pl.BlockSpec((bsz, d), lambda l, j: (0, 0)),
            pl.BlockSpec((None, d, col_tile), lambda l, j: (l, 0, j)),
            pl.BlockSpec((None, 1, col_tile), lambda l, j: (l, 0, j)),
        ],
        out_specs=pl.BlockSpec((None, bsz, col_tile), lambda l, j: (l, 0, j)),
        out_shape=jax.ShapeDtypeStruct((n_layers, bsz, n), F32),
        compiler_params=_params(2),
        name="modvec",
    )(c, w, b)


def _mixer_kernel(x_ref, mod_ref, ng_ref, win_ref, ck_ref, wout_ref, o_ref, zbuf):
    tm, d = x_ref.shape
    x = x_ref[...]
    sh1, sc1, g1 = mod_ref[0:1, :], mod_ref[1:2, :], mod_ref[2:3, :]
    h = (_rms(x, ng_ref[0:1, :]) * (1.0 + sc1) + sh1).astype(BF16)
    b_gate = _dot(h, win_ref[:, 0:d])
    z = _dot(h, win_ref[:, d:2 * d]) * _dot(h, win_ref[:, 2 * d:3 * d])

    @pl.when(pl.program_id(1) == 0)
    def _():
        zbuf[0:8, :] = jnp.zeros((8, d), F32)

    zbuf[8:8 + tm, :] = z
    conv = (ck_ref[0:1, :] * zbuf[6:6 + tm, :]
            + ck_ref[1:2, :] * zbuf[7:7 + tm, :]
            + ck_ref[2:3, :] * z)
    zbuf[0:8, :] = zbuf[tm:tm + 8, :]
    y = _dot((b_gate * conv).astype(BF16), wout_ref[...])
    o_ref[...] = x + g1 * _rms(y, ng_ref[1:2, :])


def _mixer(x, mod, norm_g, layer, w_in, conv_k, w_out, a_idx):
    bsz, s, d = x.shape
    tm = ROW_TILE
    return pl.pallas_call(
        _mixer_kernel,
        grid=(bsz, s // tm),
        in_specs=[
            pl.BlockSpec((None, tm, d), lambda b, i: (b, i, 0)),
            pl.BlockSpec((None, None, 6, d), lambda b, i: (layer, b, 0, 0)),
            _resident((None, 4, d), lambda b, i: (layer, 0, 0)),
            _resident((None, d, 3 * d), lambda b, i: (a_idx, 0, 0)),
            _resident((None, 3, d), lambda b, i: (a_idx, 0, 0)),
            _resident((None, d, d), lambda b, i: (a_idx, 0, 0)),
        ],
        out_specs=pl.BlockSpec((None, tm, d), lambda b, i: (b, i, 0)),
        out_shape=jax.ShapeDtypeStruct(x.shape, F32),
        scratch_shapes=[pltpu.VMEM((tm + 8, d), F32)],
        compiler_params=_params(2),
        name="conv_mixer",
    )(x, mod, norm_g, w_in, conv_k, w_out)


def _ffn_kernel(x_ref, mod_ref, ng_ref, win_ref, wout_ref, o_ref, *, n_split):
    x = x_ref[...]
    sh2, sc2, g2 = mod_ref[3:4, :], mod_ref[4:5, :], mod_ref[5:6, :]
    h = (_rms(x, ng_ref[2:3, :]) * (1.0 + sc2) + sh2).astype(BF16)
    f = wout_ref.shape[0]
    fc = f // n_split
    y = None
    for j in range(n_split):
        g = _dot(h, win_ref[:, j * fc:(j + 1) * fc])
        u = _dot(h, win_ref[:, f + j * fc:f + (j + 1) * fc])
        yj = _dot((jax.nn.silu(g) * u).astype(BF16), wout_ref[j * fc:(j + 1) * fc, :])
        y = yj if y is None else y + yj
    o_ref[...] = x + g2 * _rms(y, ng_ref[3:4, :])


def _ffn(x, mod, norm_g, layer, w_in, w_out):
    bsz, s, d = x.shape
    f = w_out.shape[1]
    tm = ROW_TILE
    return pl.pallas_call(
        functools.partial(_ffn_kernel, n_split=1),
        grid=(bsz, s // tm),
        in_specs=[
            pl.BlockSpec((None, tm, d), lambda b, i: (b, i, 0)),
            pl.BlockSpec((None, None, 6, d), lambda b, i: (layer, b, 0, 0)),
            _resident((None, 4, d), lambda b, i: (layer, 0, 0)),
            _resident((None, d, 2 * f), lambda b, i: (layer, 0, 0)),
            _resident((None, f, d), lambda b, i: (layer, 0, 0)),
        ],
        out_specs=pl.BlockSpec((None, tm, d), lambda b, i: (b, i, 0)),
        out_shape=jax.ShapeDtypeStruct(x.shape, F32),
        compiler_params=_params(2),
        name="swiglu_ffn",
    )(x, mod, norm_g, w_in, w_out)


def _proj_kernel(x_ref, mod_ref, kvmod_ref, ng_ref, kvg_ref, wkv_ref, wq_ref,
                 k_ref, v_ref, q_ref):
    d = x_ref.shape[1]
    x = x_ref[...]
    xn = x * lax.rsqrt(jnp.mean(x * x, axis=-1, keepdims=True) + EPS)
    hkv = ((xn * kvg_ref[...]) * (1.0 + kvmod_ref[1:2, :]) + kvmod_ref[0:1, :]).astype(BF16)
    k_ref[...] = _dot(hkv, wkv_ref[:, 0:d]).astype(BF16)
    v_ref[...] = _dot(hkv, wkv_ref[:, d:2 * d]).astype(BF16)
    h = ((xn * ng_ref[0:1, :]) * (1.0 + mod_ref[1:2, :]) + mod_ref[0:1, :]).astype(BF16)
    q_ref[...] = (_dot(h, wq_ref[...]) * (HEAD_DIM ** -0.5)).astype(BF16)


def _proj(x, mod, kvmod, norm_g, layer, kv_norm_g, w_kv, w_q, b_idx):
    bsz, s, d = x.shape
    tm = ROW_TILE
    row_spec = pl.BlockSpec((None, tm, d), lambda b, i: (b, i, 0))
    out = jax.ShapeDtypeStruct(x.shape, BF16)
    return pl.pallas_call(
        _proj_kernel,
        grid=(bsz, s // tm),
        in_specs=[
            row_spec,
            pl.BlockSpec((None, None, 6, d), lambda b, i: (layer, b, 0, 0)),
            pl.BlockSpec((None, None, 2, d), lambda b, i: (0, b, 0, 0)),
            _resident((None, 4, d), lambda b, i: (layer, 0, 0)),
            _resident((1, d), lambda b, i: (0, 0)),
            _resident((d, 2 * d), lambda b, i: (0, 0)),
            _resident((None, d, d), lambda b, i: (b_idx, 0, 0)),
        ],
        out_specs=[row_spec, row_spec, row_spec],
        out_shape=[out, out, out],
        compiler_params=_params(2),
        name="qkv_proj",
    )(x, mod, kvmod, norm_g, kv_norm_g, w_kv, w_q)


def _bias_kernel(rb_ref, o_ref):
    n_heads = rb_ref.shape[0]
    n_tab = 2 * MAX_REL
    rb = rb_ref[:, 0:n_tab]
    hi = rb.astype(BF16)
    r1 = rb - hi.astype(F32)
    mid = r1.astype(BF16)
    lo = (r1 - mid.astype(F32)).astype(BF16)
    pieces = jnp.concatenate([hi, mid, lo], axis=0)
    rb_top = rb_ref[:, n_tab:n_tab + 1]
    row = lax.broadcasted_iota(jnp.int32, (n_tab, SLAB), 0)
    col = lax.broadcasted_iota(jnp.int32, (n_tab, SLAB), 1)
    neg = jnp.finfo(F32).min
    for var in range(2):
        kk = col - var * CHUNK
        in_band = (kk >= 0) & (kk < BAND)
        base = (N_LEFT_CHUNKS - (kk >> 6)) * CHUNK - (kk & (CHUNK - 1))

        def body(a, carry, var=var, in_band=in_band, base=base):
            idx = jnp.where(in_band, jnp.clip(base + a, -MAX_REL, MAX_REL) + MAX_REL, -1)
            onehot = jnp.where(row == idx, 1.0, 0.0).astype(BF16)
            t3 = _dot(pieces, onehot)
            t = t3[0:n_heads] + t3[n_heads:2 * n_heads] + t3[2 * n_heads:3 * n_heads]
            t = t + jnp.where(idx[0:1, :] == n_tab, rb_top, 0.0)
            t = jnp.where(in_band[0:1, :], t, neg)
            for h in range(n_heads):
                o_ref[var, h, pl.ds(a, 1), :] = t[h:h + 1, :]
            return carry

        lax.fori_loop(0, CHUNK, body, 0)


def _bias_table(rel_bias, b_idx):
    n_heads, n_rel = rel_bias.shape[1:]
    return pl.pallas_call(
        _bias_kernel,
        grid=(1,),
        in_specs=[pl.BlockSpec((None, n_heads, n_rel), lambda i: (b_idx, 0, 0))],
        out_specs=pl.BlockSpec((2, n_heads, CHUNK, SLAB), lambda i: (0, 0, 0, 0)),
        out_shape=jax.ShapeDtypeStruct((2, n_heads, CHUNK, SLAB), F32),
        compiler_params=_params(1),
        name="rel_bias_table",
    )(rel_bias)


def _attn_kernel(x_ref, q_ref, kp_ref, kc_ref, vp_ref, vc_ref, bias_ref, wo_ref,
                 mod_ref, ng_ref, o_ref, kbuf, vbuf, obuf, q2buf, sbuf, pbuf):
    tm, d = q_ref.shape
    kbuf[0:tm, :] = kp_ref[...]
    kbuf[tm:2 * tm, :] = kc_ref[...]
    vbuf[0:tm, :] = vp_ref[...]
    vbuf[tm:2 * tm, :] = vc_ref[...]
    first_tile = pl.program_id(1) == 0
    low_head = lax.broadcasted_iota(jnp.int32, (CHUNK, V7X_LANES), 1) < HEAD_DIM
    neg = jnp.finfo(F32).min
    body = 0
    for g in range(tm // (GROUP * CHUNK)):
        w0 = g * GROUP * CHUNK
        for p in range(d // V7X_LANES):
            ls = slice(p * V7X_LANES, (p + 1) * V7X_LANES)
            slot = body % 2
            body += 1
            for j in range(GROUP):
                qp = q_ref[w0 + j * CHUNK:w0 + (j + 1) * CHUNK, ls]
                zero = jnp.zeros_like(qp)
                q2buf[slot, (2 * j) * CHUNK:(2 * j + 1) * CHUNK, :] = jnp.where(low_head, qp, zero)
                q2buf[slot, (2 * j + 1) * CHUNK:(2 * j + 2) * CHUNK, :] = jnp.where(low_head, zero, qp)
            sbuf[slot] = lax.dot_general(q2buf[slot], kbuf[w0:w0 + WIN, ls],
                                         (((1,), (1,)), ((), ())), preferred_element_type=F32)

            @pl.when(first_tile)
            def _(slot=slot, w0=w0):
                sbuf[slot, :, 0:tm - w0] = jnp.full((2 * GROUP * CHUNK, tm - w0), neg, F32)

            inv_l = []
            for j in range(GROUP):
                t0 = (j // 2) * V7X_LANES
                rest = slice(SLAB, WIN) if t0 == 0 else slice(0, V7X_LANES)
                for e in range(2):
                    rows = slice((2 * j + e) * CHUNK, (2 * j + e + 1) * CHUNK)
                    s = sbuf[slot, rows, t0:t0 + SLAB] + bias_ref[j % 2, p, e * CHUNK:(e + 1) * CHUNK, :]
                    ex = jnp.exp(s - jnp.max(s, axis=-1, keepdims=True))
                    inv_l.append(1.0 / jnp.sum(ex, axis=-1, keepdims=True))
                    pbuf[slot, rows, t0:t0 + SLAB] = ex.astype(BF16)
                    pbuf[slot, rows, rest] = jnp.zeros((CHUNK, V7X_LANES), BF16)
            o2 = _dot(pbuf[slot], vbuf[w0:w0 + WIN, ls])
            for j in range(GROUP):
                lo = o2[(2 * j) * CHUNK:(2 * j + 1) * CHUNK, :] * inv_l[2 * j]
                hi = o2[(2 * j + 1) * CHUNK:(2 * j + 2) * CHUNK, :] * inv_l[2 * j + 1]
                obuf[w0 + j * CHUNK:w0 + (j + 1) * CHUNK, ls] = jnp.where(low_head, lo, hi).astype(BF16)
    y = _dot(obuf[...], wo_ref[...])
    o_ref[...] = x_ref[...] + mod_ref[2:3, :] * _rms(y, ng_ref[1:2, :])


def _attn(x, q, k, v, bias, w_o, mod, norm_g, layer, b_idx):
    bsz, s, d = x.shape
    tm = ROW_TILE
    assert tm == N_LEFT_CHUNKS * CHUNK and tm % (GROUP * CHUNK) == 0
    row_spec = pl.BlockSpec((None, tm, d), lambda b, i: (b, i, 0))
    prev_spec = pl.BlockSpec((None, tm, d), lambda b, i: (b, jnp.maximum(i - 1, 0), 0))
    n_pairs = d // V7X_LANES
    m_rows = 2 * GROUP * CHUNK
    return pl.pallas_call(
        _attn_kernel,
        grid=(bsz, s // tm),
        in_specs=[
            row_spec, row_spec, prev_spec, row_spec, prev_spec, row_spec,
            _resident((2, n_pairs, 2 * CHUNK, SLAB), lambda b, i: (0, 0, 0, 0)),
            _resident((None, d, d), lambda b, i: (b_idx, 0, 0)),
            pl.BlockSpec((None, None, 6, d), lambda b, i: (layer, b, 0, 0)),
            _resident((None, 4, d), lambda b, i: (layer, 0, 0)),
        ],
        out_specs=row_spec,
        out_shape=jax.ShapeDtypeStruct(x.shape, F32),
        scratch_shapes=[pltpu.VMEM((2 * tm, d), BF16), pltpu.VMEM((2 * tm, d), BF16),
                        pltpu.VMEM((tm, d), BF16),
                        pltpu.VMEM((2, m_rows, V7X_LANES), BF16),
                        pltpu.VMEM((2, m_rows, WIN), F32),
                        pltpu.VMEM((2, m_rows, WIN), BF16)],
        compiler_params=_params(2),
        name="band_attention",
    )(x, q, k, k, v, v, bias, w_o, mod, norm_g)


def kernel(x, c, mod_w, mod_b, norm_g, ffn_w_in, ffn_w_out, conv_w_in, conv_k, conv_w_out,
           kv_mod_w, kv_mod_b, kv_norm_g, w_kv, attn_w_q, attn_w_o, rel_bias):
    bsz, s, d = x.shape
    depth = mod_w.shape[0]
    n_a = conv_w_in.shape[0]
    n_heads = rel_bias.shape[1]
    assert d // n_heads == HEAD_DIM and s % ROW_TILE == 0

    mod = _modvec(c, mod_w, mod_b.reshape(depth, 1, 6 * d)).reshape(depth, bsz, 6, d)
    kvmod = _modvec(c, kv_mod_w[None], kv_mod_b.reshape(1, 1, 2 * d)).reshape(1, bsz, 2, d)

    ffn_w_in, ffn_w_out = ffn_w_in.astype(BF16), ffn_w_out.astype(BF16)
    conv_w_in, conv_w_out = conv_w_in.astype(BF16), conv_w_out.astype(BF16)
    w_kv, attn_w_q, attn_w_o = w_kv.astype(BF16), attn_w_q.astype(BF16), attn_w_o.astype(BF16)
    kv_norm_g = kv_norm_g.reshape(1, d)

    k = v = None
    for layer in range(depth):
        if layer < n_a:
            x = _mixer(x, mod, norm_g, layer, conv_w_in, conv_k, conv_w_out, layer)
        else:
            bi = layer - n_a
            k_new, v_new, q = _proj(x, mod, kvmod, norm_g, layer, kv_norm_g, w_kv, attn_w_q, bi)
            if layer == n_a:
                k, v = k_new, v_new
            bias = _bias_table(rel_bias, bi).reshape(2, n_heads // 2, 2 * CHUNK, SLAB)
            x = _attn(x, q, k, v, bias, attn_w_o, mod, norm_g, layer, bi)
        x = _ffn(x, mod, norm_g, layer, ffn_w_in, ffn_w_out)
    return x
```

```python
import functools

import jax
import jax.numpy as jnp
from jax import lax
from jax.experimental import pallas as pl
from jax.experimental.pallas import tpu as pltpu

EPS = 1e-6
CHUNK = 64
N_LEFT_CHUNKS = 8
BAND = (N_LEFT_CHUNKS + 1) * CHUNK
SLAB = BAND + CHUNK
GROUP = 4
WIN = (GROUP + N_LEFT_CHUNKS) * CHUNK
HEAD_DIM = 64
MAX_REL = 2 * CHUNK

V7X_LANES = 128
ROW_TILE = 512
V7X_VMEM_LIMIT = 56 * 1024 * 1024

BF16 = jnp.bfloat16
F32 = jnp.float32


def _dot(a, b):
    return jnp.dot(a, b, preferred_element_type=F32)


def _rms(x, g):
    return x * lax.rsqrt(jnp.mean(x * x, axis=-1, keepdims=True) + EPS) * g


def _resident(block_shape, index_map):
    return pl.BlockSpec(block_shape, index_map, pipeline_mode=pl.Buffered(1))


def _params(n_grid_dims):
    return pltpu.CompilerParams(
        dimension_semantics=("arbitrary",) * n_grid_dims,
        vmem_limit_bytes=V7X_VMEM_LIMIT)


def _modvec_kernel(c_ref, w_ref, b_ref, o_ref):
    s = jax.nn.silu(c_ref[...]).astype(BF16)
    o_ref[...] = _dot(s, w_ref[...].astype(BF16)) + b_ref[...]


def _modvec(c, w, b, col_tile=1024):
    n_layers, d, n = w.shape
    bsz = c.shape[0]
    return pl.pallas_call(
        _modvec_kernel,
        grid=(n_layers, n // col_tile),
        in_specs=[
            pl.BlockSpec((bsz, d), lambda l, j: (0, 0)),
            pl.BlockSpec((None, d, col_tile), lambda l, j: (l, 0, j)),
            pl.BlockSpec((None, 1, col_tile), lambda l, j: (l, 0, j)),
        ],
        out_specs=pl.BlockSpec((None, bsz, col_tile), lambda l, j: (l, 0, j)),
        out_shape=jax.ShapeDtypeStruct((n_layers, bsz, n), F32),
        compiler_params=_params(2),
        name="modvec",
    )(c, w, b)


def _mixer_kernel(x_ref, mod_ref, ng_ref, win_ref, ck_ref, wout_ref, o_ref, zbuf):
    tm, d = x_ref.shape
    x = x_ref[...]
    sh1, sc1, g1 = mod_ref[0:1, :], mod_ref[1:2, :], mod_ref[2:3, :]
    h = (_rms(x, ng_ref[0:1, :]) * (1.0 + sc1) + sh1).astype(BF16)
    b_gate = _dot(h, win_ref[:, 0:d])
    z = _dot(h, win_ref[:, d:2 * d]) * _dot(h, win_ref[:, 2 * d:3 * d])

    @pl.when(pl.program_id(1) == 0)
    def _():
        zbuf[0:8, :] = jnp.zeros((8, d), F32)

    zbuf[8:8 + tm, :] = z
    conv = (ck_ref[0:1, :] * zbuf[6:6 + tm, :]
            + ck_ref[1:2, :] * zbuf[7:7 + tm, :]
            + ck_ref[2:3, :] * z)
    zbuf[0:8, :] = zbuf[tm:tm + 8, :]
    y = _dot((b_gate * conv).astype(BF16), wout_ref[...])
    o_ref[...] = x + g1 * _rms(y, ng_ref[1:2, :])


def _mixer(x, mod, norm_g, layer, w_in, conv_k, w_out, a_idx):
    bsz, s, d = x.shape
    tm = ROW_TILE
    return pl.pallas_call(
        _mixer_kernel,
        grid=(bsz, s // tm),
        in_specs=[
            pl.BlockSpec((None, tm, d), lambda b, i: (b, i, 0)),
            pl.BlockSpec((None, None, 6, d), lambda b, i: (layer, b, 0, 0)),
            _resident((None, 4, d), lambda b, i: (layer, 0, 0)),
            _resident((None, d, 3 * d), lambda b, i: (a_idx, 0, 0)),
            _resident((None, 3, d), lambda b, i: (a_idx, 0, 0)),
            _resident((None, d, d), lambda b, i: (a_idx, 0, 0)),
        ],
        out_specs=pl.BlockSpec((None, tm, d), lambda b, i: (b, i, 0)),
        out_shape=jax.ShapeDtypeStruct(x.shape, F32),
        scratch_shapes=[pltpu.VMEM((tm + 8, d), F32)],
        compiler_params=_params(2),
        name="conv_mixer",
    )(x, mod, norm_g, w_in, conv_k, w_out)


def _ffn_kernel(x_ref, mod_ref, ng_ref, win_ref, wout_ref, o_ref, *, n_split):
    x = x_ref[...]
    sh2, sc2, g2 = mod_ref[3:4, :], mod_ref[4:5, :], mod_ref[5:6, :]
    h = (_rms(x, ng_ref[2:3, :]) * (1.0 + sc2) + sh2).astype(BF16)
    f = wout_ref.shape[0]
    fc = f // n_split
    y = None
    for j in range(n_split):
        g = _dot(h, win_ref[:, j * fc:(j + 1) * fc])
        u = _dot(h, win_ref[:, f + j * fc:f + (j + 1) * fc])
        yj = _dot((jax.nn.silu(g) * u).astype(BF16), wout_ref[j * fc:(j + 1) * fc, :])
        y = yj if y is None else y + yj
    o_ref[...] = x + g2 * _rms(y, ng_ref[3:4, :])


def _ffn(x, mod, norm_g, layer, w_in, w_out):
    bsz, s, d = x.shape
    f = w_out.shape[1]
    tm = ROW_TILE
    return pl.pallas_call(
        functools.partial(_ffn_kernel, n_split=1),
        grid=(bsz, s // tm),
        in_specs=[
            pl.BlockSpec((None, tm, d), lambda b, i: (b, i, 0)),
            pl.BlockSpec((None, None, 6, d), lambda b, i: (layer, b, 0, 0)),
            _resident((None, 4, d), lambda b, i: (layer, 0, 0)),
            _resident((None, d, 2 * f), lambda b, i: (layer, 0, 0)),
            _resident((None, f, d), lambda b, i: (layer, 0, 0)),
        ],
        out_specs=pl.BlockSpec((None, tm, d), lambda b, i: (b, i, 0)),
        out_shape=jax.ShapeDtypeStruct(x.shape, F32),
        compiler_params=_params(2),
        name="swiglu_ffn",
    )(x, mod, norm_g, w_in, w_out)


def _proj_kernel(x_ref, mod_ref, kvmod_ref, ng_ref, kvg_ref, wkv_ref, wq_ref,
                 k_ref, v_ref, q_ref):
    d = x_ref.shape[1]
    x = x_ref[...]
    xn = x * lax.rsqrt(jnp.mean(x * x, axis=-1, keepdims=True) + EPS)
    hkv = ((xn * kvg_ref[...]) * (1.0 + kvmod_ref[1:2, :]) + kvmod_ref[0:1, :]).astype(BF16)
    k_ref[...] = _dot(hkv, wkv_ref[:, 0:d]).astype(BF16)
    v_ref[...] = _dot(hkv, wkv_ref[:, d:2 * d]).astype(BF16)
    h = ((xn * ng_ref[0:1, :]) * (1.0 + mod_ref[1:2, :]) + mod_ref[0:1, :]).astype(BF16)
    q_ref[...] = (_dot(h, wq_ref[...]) * (HEAD_DIM ** -0.5)).astype(BF16)


def _proj(x, mod, kvmod, norm_g, layer, kv_norm_g, w_kv, w_q, b_idx):
    bsz, s, d = x.shape
    tm = ROW_TILE
    row_spec = pl.BlockSpec((None, tm, d), lambda b, i: (b, i, 0))
    out = jax.ShapeDtypeStruct(x.shape, BF16)
    return pl.pallas_call(
        _proj_kernel,
        grid=(bsz, s // tm),
        in_specs=[
            row_spec,
            pl.BlockSpec((None, None, 6, d), lambda b, i: (layer, b, 0, 0)),
            pl.BlockSpec((None, None, 2, d), lambda b, i: (0, b, 0, 0)),
            _resident((None, 4, d), lambda b, i: (layer, 0, 0)),
            _resident((1, d), lambda b, i: (0, 0)),
            _resident((d, 2 * d), lambda b, i: (0, 0)),
            _resident((None, d, d), lambda b, i: (b_idx, 0, 0)),
        ],
        out_specs=[row_spec, row_spec, row_spec],
        out_shape=[out, out, out],
        compiler_params=_params(2),
        name="qkv_proj",
    )(x, mod, kvmod, norm_g, kv_norm_g, w_kv, w_q)


def _bias_kernel(rb_ref, o_ref):
    n_heads = rb_ref.shape[0]
    n_tab = 2 * MAX_REL
    rb = rb_ref[:, 0:n_tab]
    hi = rb.astype(BF16)
    r1 = rb - hi.astype(F32)
    mid = r1.astype(BF16)
    lo = (r1 - mid.astype(F32)).astype(BF16)
    pieces = jnp.concatenate([hi, mid, lo], axis=0)
    rb_top = rb_ref[:, n_tab:n_tab + 1]
    row = lax.broadcasted_iota(jnp.int32, (n_tab, SLAB), 0)
    col = lax.broadcasted_iota(jnp.int32, (n_tab, SLAB), 1)
    neg = jnp.finfo(F32).min
    for var in range(2):
        kk = col - var * CHUNK
        in_band = (kk >= 0) & (kk < BAND)
        base = (N_LEFT_CHUNKS - (kk >> 6)) * CHUNK - (kk & (CHUNK - 1))

        def body(a, carry, var=var, in_band=in_band, base=base):
            idx = jnp.where(in_band, jnp.clip(base + a, -MAX_REL, MAX_REL) + MAX_REL, -1)
            onehot = jnp.where(row == idx, 1.0, 0.0).astype(BF16)
            t3 = _dot(pieces, onehot)
            t = t3[0:n_heads] + t3[n_heads:2 * n_heads] + t3[2 * n_heads:3 * n_heads]
            t = t + jnp.where(idx[0:1, :] == n_tab, rb_top, 0.0)
            t = jnp.where(in_band[0:1, :], t, neg)
            for h in range(n_heads):
                o_ref[var, h, pl.ds(a, 1), :] = t[h:h + 1, :]
            return carry

        lax.fori_loop(0, CHUNK, body, 0)


def _bias_table(rel_bias, b_idx):
    n_heads, n_rel = rel_bias.shape[1:]
    return pl.pallas_call(
        _bias_kernel,
        grid=(1,),
        in_specs=[pl.BlockSpec((None, n_heads, n_rel), lambda i: (b_idx, 0, 0))],
        out_specs=pl.BlockSpec((2, n_heads, CHUNK, SLAB), lambda i: (0, 0, 0, 0)),
        out_shape=jax.ShapeDtypeStruct((2, n_heads, CHUNK, SLAB), F32),
        compiler_params=_params(1),
        name="rel_bias_table",
    )(rel_bias)


def _attn_kernel(*refs, first_tile):
    if first_tile:
        (x_ref, q_ref, k_ref, v_ref, bias_ref, wo_ref, mod_ref, ng_ref, o_ref,
         obuf, q2buf, sbuf, pbuf, kwin, vwin) = refs
        tm, d = q_ref.shape
        for src, win in ((k_ref, kwin), (v_ref, vwin)):
            win[0:tm, :] = jnp.zeros((tm, d), BF16)
            win[tm:2 * tm, :] = src[...]
    else:
        (x_ref, q_ref, kwin, vwin, bias_ref, wo_ref, mod_ref, ng_ref, _, o_ref,
         obuf, q2buf, sbuf, pbuf) = refs
        kwin, vwin = kwin.at[0], vwin.at[0]
        tm, d = q_ref.shape
    low_head = lax.broadcasted_iota(jnp.int32, (CHUNK, V7X_LANES), 1) < HEAD_DIM
    neg = jnp.finfo(F32).min
    half = GROUP * CHUNK
    for slot in range(2):
        pbuf[slot, 0:half, SLAB:WIN] = jnp.zeros((half, WIN - SLAB), BF16)
        pbuf[slot, half:2 * half, 0:V7X_LANES] = jnp.zeros((half, V7X_LANES), BF16)
    body = 0
    for g in range(tm // (GROUP * CHUNK)):
        w0 = g * GROUP * CHUNK
        for p in range(d // V7X_LANES):
            ls = slice(p * V7X_LANES, (p + 1) * V7X_LANES)
            slot = body % 2
            body += 1
            for j in range(GROUP):
                qp = q_ref[w0 + j * CHUNK:w0 + (j + 1) * CHUNK, ls]
                zero = jnp.zeros_like(qp)
                q2buf[slot, (2 * j) * CHUNK:(2 * j + 1) * CHUNK, :] = jnp.where(low_head, qp, zero)
                q2buf[slot, (2 * j + 1) * CHUNK:(2 * j + 2) * CHUNK, :] = jnp.where(low_head, zero, qp)
            sbuf[slot] = lax.dot_general(q2buf[slot], kwin[w0:w0 + WIN, ls],
                                         (((1,), (1,)), ((), ())), preferred_element_type=F32)
            if first_tile:
                sbuf[slot, :, 0:tm - w0] = jnp.full((2 * half, tm - w0), neg, F32)
            inv_l = []
            for j in range(GROUP):
                t0 = (j // 2) * V7X_LANES
                for e in range(2):
                    rows = slice((2 * j + e) * CHUNK, (2 * j + e + 1) * CHUNK)
                    s = sbuf[slot, rows, t0:t0 + SLAB] + bias_ref[j % 2, p, e * CHUNK:(e + 1) * CHUNK, :]
                    ex = jnp.exp(s - jnp.max(s, axis=-1, keepdims=True))
                    inv_l.append(1.0 / jnp.sum(ex, axis=-1, keepdims=True))
                    pbuf[slot, rows, t0:t0 + SLAB] = ex.astype(BF16)
            o2 = _dot(pbuf[slot], vwin[w0:w0 + WIN, ls])
            for j in range(GROUP):
                lo = o2[(2 * j) * CHUNK:(2 * j + 1) * CHUNK, :] * inv_l[2 * j]
                hi = o2[(2 * j + 1) * CHUNK:(2 * j + 2) * CHUNK, :] * inv_l[2 * j + 1]
                obuf[w0 + j * CHUNK:w0 + (j + 1) * CHUNK, ls] = jnp.where(low_head, lo, hi).astype(BF16)
    y = _dot(obuf[...], wo_ref[...])
    o_ref[...] = x_ref[...] + mod_ref[2:3, :] * _rms(y, ng_ref[1:2, :])


def _attn(x, q, k, v, bias, w_o, mod, norm_g, layer, b_idx):
    bsz, s, d = x.shape
    tm = ROW_TILE
    n_tiles = s // tm
    assert tm == N_LEFT_CHUNKS * CHUNK and tm % (GROUP * CHUNK) == 0
    n_pairs = d // V7X_LANES
    m_rows = 2 * GROUP * CHUNK
    scratch = [pltpu.VMEM((tm, d), BF16),
               pltpu.VMEM((2, m_rows, V7X_LANES), BF16),
               pltpu.VMEM((2, m_rows, WIN), F32),
               pltpu.VMEM((2, m_rows, WIN), BF16)]

    def call(first_tile, out_so_far=None):
        t_off = 0 if first_tile else 1
        row_spec = pl.BlockSpec((None, tm, d), lambda b, i: (b, i + t_off, 0))
        if first_tile:
            kv_spec = row_spec
        else:
            kv_spec = pl.BlockSpec((pl.Element(1), pl.Element(2 * tm), pl.Element(d)),
                                   lambda b, i: (b, i * tm, 0))
        in_specs = [
            row_spec, row_spec, kv_spec, kv_spec,
            _resident((2, n_pairs, 2 * CHUNK, SLAB), lambda b, i: (0, 0, 0, 0)),
            _resident((None, d, d), lambda b, i: (b_idx, 0, 0)),
            pl.BlockSpec((None, None, 6, d), lambda b, i: (layer, b, 0, 0)),
            _resident((None, 4, d), lambda b, i: (layer, 0, 0)),
        ]
        args = [x, q, k, v, bias, w_o, mod, norm_g]
        if first_tile:
            kv_scratch = [pltpu.VMEM((2 * tm, d), BF16), pltpu.VMEM((2 * tm, d), BF16)]
            aliases = {}
        else:
            in_specs.append(pl.BlockSpec(memory_space=pl.ANY))
            args.append(out_so_far)
            kv_scratch = []
            aliases = {len(args) - 1: 0}
        return pl.pallas_call(
            functools.partial(_attn_kernel, first_tile=first_tile),
            grid=(bsz, 1 if first_tile else n_tiles - 1),
            in_specs=in_specs,
            out_specs=row_spec,
            out_shape=jax.ShapeDtypeStruct(x.shape, F32),
            scratch_shapes=scratch + kv_scratch,
            input_output_aliases=aliases,
            compiler_params=_params(2),
            name="band_attention_first" if first_tile else "band_attention",
        )(*args)

    return call(False, call(True))


def kernel(x, c, mod_w, mod_b, norm_g, ffn_w_in, ffn_w_out, conv_w_in, conv_k, conv_w_out,
           kv_mod_w, kv_mod_b, kv_norm_g, w_kv, attn_w_q, attn_w_o, rel_bias):
    bsz, s, d = x.shape
    depth = mod_w.shape[0]
    n_a = conv_w_in.shape[0]
    n_heads = rel_bias.shape[1]
    assert d // n_heads == HEAD_DIM and s % ROW_TILE == 0

    mod = _modvec(c, mod_w, mod_b.reshape(depth, 1, 6 * d)).reshape(depth, bsz, 6, d)
    kvmod = _modvec(c, kv_mod_w[None], kv_mod_b.reshape(1, 1, 2 * d)).reshape(1, bsz, 2, d)

    ffn_w_in, ffn_w_out = ffn_w_in.astype(BF16), ffn_w_out.astype(BF16)
    conv_w_in, conv_w_out = conv_w_in.astype(BF16), conv_w_out.astype(BF16)
    w_kv, attn_w_q, attn_w_o = w_kv.astype(BF16), attn_w_q.astype(BF16), attn_w_o.astype(BF16)
    kv_norm_g = kv_norm_g.reshape(1, d)

    k = v = None
    for layer in range(depth):
        if layer < n_a:
            x = _mixer(x, mod, norm_g, layer, conv_w_in, conv_k, conv_w_out, layer)
        else:
            bi = layer - n_a
            k_new, v_new, q = _proj(x, mod, kvmod, norm_g, layer, kv_norm_g, w_kv, attn_w_q, bi)
            if layer == n_a:
                k, v = k_new, v_new
            bias = _bias_table(rel_bias, bi).reshape(2, n_heads // 2, 2 * CHUNK, SLAB)
            x = _attn(x, q, k, v, bias, attn_w_o, mod, norm_g, layer, bi)
        x = _ffn(x, mod, norm_g, layer, ffn_w_in, ffn_w_out)
    return x
```

```python
import functools

import jax
import jax.numpy as jnp
from jax import lax
from jax.experimental import pallas as pl
from jax.experimental.pallas import tpu as pltpu

EPS = 1e-6
CHUNK = 64
N_LEFT_CHUNKS = 8
BAND = (N_LEFT_CHUNKS + 1) * CHUNK
SLAB = BAND + CHUNK
GROUP = 4
WIN = (GROUP + N_LEFT_CHUNKS) * CHUNK
HEAD_DIM = 64
MAX_REL = 2 * CHUNK

V7X_LANES = 128
V7X_BF16_SUBLANES = 16
ROW_TILE = 512
V7X_VMEM_LIMIT = 56 * 1024 * 1024

BF16 = jnp.bfloat16
F32 = jnp.float32


def _dot(a, b):
    return jnp.dot(a, b, preferred_element_type=F32)


def _rms(x, g):
    return x * lax.rsqrt(jnp.mean(x * x, axis=-1, keepdims=True) + EPS) * g


def _resident(block_shape, index_map):
    return pl.BlockSpec(block_shape, index_map, pipeline_mode=pl.Buffered(1))


def _params(n_grid_dims):
    return pltpu.CompilerParams(
        dimension_semantics=("arbitrary",) * n_grid_dims,
        vmem_limit_bytes=V7X_VMEM_LIMIT)


def _with_casts(body, n_in, n_out, n_cast):
    def wrapped(*refs):
        srcs = refs[n_in:n_in + n_cast]
        outs_end = n_in + n_cast + n_out
        dsts = refs[outs_end:outs_end + n_cast]
        for src, dst in zip(srcs, dsts):
            dst[...] = src[...].astype(BF16)
        body(*refs[:n_in], *refs[n_in + n_cast:outs_end], *refs[outs_end + n_cast:])
    return wrapped


def _tiled_call(body, name, grid, in_specs, args, out_specs, out_shapes, scratch=(), casts=()):
    bsz, n_tiles = grid
    c_in, c_out, c_shapes, c_args = [], [], [], []
    for w, idx in casts:
        r, c = w.shape[1:]
        rows, cols = r // n_tiles, c // bsz
        assert rows * n_tiles == r and rows % V7X_BF16_SUBLANES == 0
        assert cols * bsz == c and cols % V7X_LANES == 0
        c_in.append(pl.BlockSpec((None, rows, cols), lambda b, i, idx=idx: (idx, i, b)))
        c_out.append(pl.BlockSpec((rows, cols), lambda b, i: (i, b)))
        c_shapes.append(jax.ShapeDtypeStruct((r, c), BF16))
        c_args.append(w)
    res = pl.pallas_call(
        _with_casts(body, len(in_specs), len(out_specs), len(casts)),
        grid=grid,
        in_specs=list(in_specs) + c_in,
        out_specs=list(out_specs) + c_out,
        out_shape=list(out_shapes) + c_shapes,
        scratch_shapes=list(scratch),
        compiler_params=_params(len(grid)),
        name=name,
    )(*args, *c_args)
    return res[:len(out_specs)], res[len(out_specs):]


def _modvec_kernel(c_ref, w_ref, b_ref, o_ref):
    s = jax.nn.silu(c_ref[...]).astype(BF16)
    o_ref[...] = _dot(s, w_ref[...].astype(BF16)) + b_ref[...]


def _modvec(c, w, b, col_tile=1024):
    n_layers, d, n = w.shape
    bsz = c.shape[0]
    return pl.pallas_call(
        _modvec_kernel,
        grid=(n_layers, n // col_tile),
        in_specs=[
            pl.BlockSpec((bsz, d), lambda l, j: (0, 0)),
            pl.BlockSpec((None, d, col_tile), lambda l, j: (l, 0, j)),
            pl.BlockSpec((None, 1, col_tile), lambda l, j: (l, 0, j)),
        ],
        out_specs=pl.BlockSpec((None, bsz, col_tile), lambda l, j: (l, 0, j)),
        out_shape=jax.ShapeDtypeStruct((n_layers, bsz, n), F32),
        compiler_params=_params(2),
        name="modvec",
    )(c, w, b)


def _mixer_kernel(x_ref, mod_ref, ng_ref, win_ref, ck_ref, wout_ref, o_ref, zbuf):
    tm, d = x_ref.shape
    x = x_ref[...]
    sh1, sc1, g1 = mod_ref[0:1, :], mod_ref[1:2, :], mod_ref[2:3, :]
    h = (_rms(x, ng_ref[0:1, :]) * (1.0 + sc1) + sh1).astype(BF16)
    b_gate = _dot(h, win_ref[:, 0:d])
    z = _dot(h, win_ref[:, d:2 * d]) * _dot(h, win_ref[:, 2 * d:3 * d])

    @pl.when(pl.program_id(1) == 0)
    def _():
        zbuf[0:8, :] = jnp.zeros((8, d), F32)

    zbuf[8:8 + tm, :] = z
    conv = (ck_ref[0:1, :] * zbuf[6:6 + tm, :]
            + ck_ref[1:2, :] * zbuf[7:7 + tm, :]
            + ck_ref[2:3, :] * z)
    zbuf[0:8, :] = zbuf[tm:tm + 8, :]
    y = _dot((b_gate * conv).astype(BF16), wout_ref[...])
    o_ref[...] = x + g1 * _rms(y, ng_ref[1:2, :])


def _mixer(x, mod, norm_g, layer, w_in, conv_k, w_out, a_idx, casts):
    bsz, s, d = x.shape
    tm = ROW_TILE
    row_spec = pl.BlockSpec((None, tm, d), lambda b, i: (b, i, 0))
    (out,), cast = _tiled_call(
        _mixer_kernel, "conv_mixer", (bsz, s // tm),
        in_specs=[
            row_spec,
            pl.BlockSpec((None, None, 6, d), lambda b, i: (layer, b, 0, 0)),
            _resident((None, 4, d), lambda b, i: (layer, 0, 0)),
            _resident((None, d, 3 * d), lambda b, i: (a_idx, 0, 0)),
            _resident((None, 3, d), lambda b, i: (a_idx, 0, 0)),
            _resident((None, d, d), lambda b, i: (a_idx, 0, 0)),
        ],
        args=(x, mod, norm_g, w_in, conv_k, w_out),
        out_specs=[row_spec],
        out_shapes=[jax.ShapeDtypeStruct(x.shape, F32)],
        scratch=[pltpu.VMEM((tm + 8, d), F32)],
        casts=casts)
    return out, cast


def _ffn_kernel(x_ref, mod_ref, ng_ref, win_ref, wout_ref, o_ref, *, n_split):
    x = x_ref[...]
    sh2, sc2, g2 = mod_ref[3:4, :], mod_ref[4:5, :], mod_ref[5:6, :]
    h = (_rms(x, ng_ref[2:3, :]) * (1.0 + sc2) + sh2).astype(BF16)
    f = wout_ref.shape[0]
    fc = f // n_split
    y = None
    for j in range(n_split):
        g = _dot(h, win_ref[:, j * fc:(j + 1) * fc])
        u = _dot(h, win_ref[:, f + j * fc:f + (j + 1) * fc])
        yj = _dot((jax.nn.silu(g) * u).astype(BF16), wout_ref[j * fc:(j + 1) * fc, :])
        y = yj if y is None else y + yj
    o_ref[...] = x + g2 * _rms(y, ng_ref[3:4, :])


def _ffn(x, mod, norm_g, layer, w_in, w_out, casts):
    bsz, s, d = x.shape
    f = w_out.shape[0]
    tm = ROW_TILE
    row_spec = pl.BlockSpec((None, tm, d), lambda b, i: (b, i, 0))
    (out,), cast = _tiled_call(
        functools.partial(_ffn_kernel, n_split=1), "swiglu_ffn", (bsz, s // tm),
        in_specs=[
            row_spec,
            pl.BlockSpec((None, None, 6, d), lambda b, i: (layer, b, 0, 0)),
            _resident((None, 4, d), lambda b, i: (layer, 0, 0)),
            _resident((d, 2 * f), lambda b, i: (0, 0)),
            _resident((f, d), lambda b, i: (0, 0)),
        ],
        args=(x, mod, norm_g, w_in, w_out),
        out_specs=[row_spec],
        out_shapes=[jax.ShapeDtypeStruct(x.shape, F32)],
        casts=casts)
    return out, cast


def _proj_kernel(x_ref, mod_ref, kvmod_ref, ng_ref, kvg_ref, wkv_ref, wq_ref,
                 k_ref, v_ref, q_ref):
    d = x_ref.shape[1]
    x = x_ref[...]
    xn = x * lax.rsqrt(jnp.mean(x * x, axis=-1, keepdims=True) + EPS)
    hkv = ((xn * kvg_ref[...]) * (1.0 + kvmod_ref[1:2, :]) + kvmod_ref[0:1, :]).astype(BF16)
    k_ref[...] = _dot(hkv, wkv_ref[:, 0:d]).astype(BF16)
    v_ref[...] = _dot(hkv, wkv_ref[:, d:2 * d]).astype(BF16)
    h = ((xn * ng_ref[0:1, :]) * (1.0 + mod_ref[1:2, :]) + mod_ref[0:1, :]).astype(BF16)
    q_ref[...] = (_dot(h, wq_ref[...]) * (HEAD_DIM ** -0.5)).astype(BF16)


def _proj(x, mod, kvmod, norm_g, layer, kv_norm_g, w_kv, w_q, casts):
    bsz, s, d = x.shape
    tm = ROW_TILE
    row_spec = pl.BlockSpec((None, tm, d), lambda b, i: (b, i, 0))
    out = jax.ShapeDtypeStruct(x.shape, BF16)
    return _tiled_call(
        _proj_kernel, "qkv_proj", (bsz, s // tm),
        in_specs=[
            row_spec,
            pl.BlockSpec((None, None, 6, d), lambda b, i: (layer, b, 0, 0)),
            pl.BlockSpec((None, None, 2, d), lambda b, i: (0, b, 0, 0)),
            _resident((None, 4, d), lambda b, i: (layer, 0, 0)),
            _resident((1, d), lambda b, i: (0, 0)),
            _resident((d, 2 * d), lambda b, i: (0, 0)),
            _resident((d, d), lambda b, i: (0, 0)),
        ],
        args=(x, mod, kvmod, norm_g, kv_norm_g, w_kv, w_q),
        out_specs=[row_spec, row_spec, row_spec],
        out_shapes=[out, out, out],
        casts=casts)


def _bias_kernel(rb_ref, o_ref):
    n_heads = rb_ref.shape[0]
    n_tab = 2 * MAX_REL
    rb = rb_ref[:, 0:n_tab]
    hi = rb.astype(BF16)
    r1 = rb - hi.astype(F32)
    mid = r1.astype(BF16)
    lo = (r1 - mid.astype(F32)).astype(BF16)
    pieces = jnp.concatenate([hi, mid, lo], axis=0)
    rb_top = rb_ref[:, n_tab:n_tab + 1]
    u = lax.broadcasted_iota(jnp.int32, (1, WIN), 1)
    m = jnp.where(u >= WIN - CHUNK, u - WIN, u)
    idx = jnp.clip(N_LEFT_CHUNKS * CHUNK - m, -MAX_REL, MAX_REL) + MAX_REL
    row = lax.broadcasted_iota(jnp.int32, (n_tab, WIN), 0)
    onehot = jnp.where(row == idx, 1.0, 0.0).astype(BF16)
    t3 = _dot(pieces, onehot)
    w = t3[0:n_heads] + t3[n_heads:2 * n_heads] + t3[2 * n_heads:3 * n_heads]
    w = w + jnp.where(idx == n_tab, rb_top, 0.0)
    col = lax.broadcasted_iota(jnp.int32, (CHUNK, WIN), 1)
    neg = jnp.finfo(F32).min
    for var in range(2):
        kk = col - var * CHUNK
        in_band = (kk >= 0) & (kk < BAND)
        for h in range(n_heads):
            t = pltpu.roll(jnp.broadcast_to(w[h:h + 1, :], (CHUNK, WIN)), var * CHUNK, 1,
                           stride=1, stride_axis=0)
            o_ref[var, h] = jnp.where(in_band, t, neg)[:, 0:SLAB]


def _bias_table(rel_bias, b_idx):
    n_heads, n_rel = rel_bias.shape[1:]
    return pl.pallas_call(
        _bias_kernel,
        grid=(1,),
        in_specs=[pl.BlockSpec((None, n_heads, n_rel), lambda i: (b_idx, 0, 0))],
        out_specs=pl.BlockSpec((2, n_heads, CHUNK, SLAB), lambda i: (0, 0, 0, 0)),
        out_shape=jax.ShapeDtypeStruct((2, n_heads, CHUNK, SLAB), F32),
        compiler_params=_params(1),
        name="rel_bias_table",
    )(rel_bias)


def _attn_kernel(*refs, first_tile):
    if first_tile:
        (x_ref, q_ref, k_ref, v_ref, bias_ref, wo_ref, mod_ref, ng_ref, o_ref,
         obuf, q2buf, sbuf, pbuf, kwin, vwin) = refs
        tm, d = q_ref.shape
        for src, win in ((k_ref, kwin), (v_ref, vwin)):
            win[0:tm, :] = jnp.zeros((tm, d), BF16)
            win[tm:2 * tm, :] = src[...]
    else:
        (x_ref, q_ref, kwin, vwin, bias_ref, wo_ref, mod_ref, ng_ref, _, o_ref,
         obuf, q2buf, sbuf, pbuf) = refs
        kwin, vwin = kwin.at[0], vwin.at[0]
        tm, d = q_ref.shape
    low_head = lax.broadcasted_iota(jnp.int32, (CHUNK, V7X_LANES), 1) < HEAD_DIM
    neg = jnp.finfo(F32).min
    half = GROUP * CHUNK
    for slot in range(2):
        pbuf[slot, 0:half, SLAB:WIN] = jnp.zeros((half, WIN - SLAB), BF16)
        pbuf[slot, half:2 * half, 0:V7X_LANES] = jnp.zeros((half, V7X_LANES), BF16)
    body = 0
    for g in range(tm // (GROUP * CHUNK)):
        w0 = g * GROUP * CHUNK
        for p in range(d // V7X_LANES):
            ls = slice(p * V7X_LANES, (p + 1) * V7X_LANES)
            slot = body % 2
            body += 1
            for j in range(GROUP):
                qp = q_ref[w0 + j * CHUNK:w0 + (j + 1) * CHUNK, ls]
                zero = jnp.zeros_like(qp)
                q2buf[slot, (2 * j) * CHUNK:(2 * j + 1) * CHUNK, :] = jnp.where(low_head, qp, zero)
                q2buf[slot, (2 * j + 1) * CHUNK:(2 * j + 2) * CHUNK, :] = jnp.where(low_head, zero, qp)
            sbuf[slot] = lax.dot_general(q2buf[slot], kwin[w0:w0 + WIN, ls],
                                         (((1,), (1,)), ((), ())), preferred_element_type=F32)
            if first_tile:
                sbuf[slot, :, 0:tm - w0] = jnp.full((2 * half, tm - w0), neg, F32)
            inv_l = []
            for j in range(GROUP):
                t0 = (j // 2) * V7X_LANES
                for e in range(2):
                    rows = slice((2 * j + e) * CHUNK, (2 * j + e + 1) * CHUNK)
                    s = sbuf[slot, rows, t0:t0 + SLAB] + bias_ref[j % 2, p, e * CHUNK:(e + 1) * CHUNK, :]
                    ex = jnp.exp(s - jnp.max(s, axis=-1, keepdims=True))
                    inv_l.append(1.0 / jnp.sum(ex, axis=-1, keepdims=True))
                    pbuf[slot, rows, t0:t0 + SLAB] = ex.astype(BF16)
            o2 = _dot(pbuf[slot], vwin[w0:w0 + WIN, ls])
            for j in range(GROUP):
                lo = o2[(2 * j) * CHUNK:(2 * j + 1) * CHUNK, :] * inv_l[2 * j]
                hi = o2[(2 * j + 1) * CHUNK:(2 * j + 2) * CHUNK, :] * inv_l[2 * j + 1]
                obuf[w0 + j * CHUNK:w0 + (j + 1) * CHUNK, ls] = jnp.where(low_head, lo, hi).astype(BF16)
    y = _dot(obuf[...], wo_ref[...])
    o_ref[...] = x_ref[...] + mod_ref[2:3, :] * _rms(y, ng_ref[1:2, :])


def _attn(x, q, k, v, bias, w_o, mod, norm_g, layer):
    bsz, s, d = x.shape
    tm = ROW_TILE
    n_tiles = s // tm
    assert tm == N_LEFT_CHUNKS * CHUNK and tm % (GROUP * CHUNK) == 0
    n_pairs = d // V7X_LANES
    m_rows = 2 * GROUP * CHUNK
    scratch = [pltpu.VMEM((tm, d), BF16),
               pltpu.VMEM((2, m_rows, V7X_LANES), BF16),
               pltpu.VMEM((2, m_rows, WIN), F32),
               pltpu.VMEM((2, m_rows, WIN), BF16)]

    def call(first_tile, out_so_far=None):
        t_off = 0 if first_tile else 1
        row_spec = pl.BlockSpec((None, tm, d), lambda b, i: (b, i + t_off, 0))
        if first_tile:
            kv_spec = row_spec
        else:
            kv_spec = pl.BlockSpec((pl.Element(1), pl.Element(2 * tm), pl.Element(d)),
                                   lambda b, i: (b, i * tm, 0))
        in_specs = [
            row_spec, row_spec, kv_spec, kv_spec,
            _resident((2, n_pairs, 2 * CHUNK, SLAB), lambda b, i: (0, 0, 0, 0)),
            _resident((d, d), lambda b, i: (0, 0)),
            pl.BlockSpec((None, None, 6, d), lambda b, i: (layer, b, 0, 0)),
            _resident((None, 4, d), lambda b, i: (layer, 0, 0)),
        ]
        args = [x, q, k, v, bias, w_o, mod, norm_g]
        if first_tile:
            kv_scratch = [pltpu.VMEM((2 * tm, d), BF16), pltpu.VMEM((2 * tm, d), BF16)]
            aliases = {}
        else:
            in_specs.append(pl.BlockSpec(memory_space=pl.ANY))
            args.append(out_so_far)
            kv_scratch = []
            aliases = {len(args) - 1: 0}
        return pl.pallas_call(
            functools.partial(_attn_kernel, first_tile=first_tile),
            grid=(bsz, 1 if first_tile else n_tiles - 1),
            in_specs=in_specs,
            out_specs=row_spec,
            out_shape=jax.ShapeDtypeStruct(x.shape, F32),
            scratch_shapes=scratch + kv_scratch,
            input_output_aliases=aliases,
            compiler_params=_params(2),
            name="band_attention_first" if first_tile else "band_attention",
        )(*args)

    return call(False, call(True))


def kernel(x, c, mod_w, mod_b, norm_g, ffn_w_in, ffn_w_out, conv_w_in, conv_k, conv_w_out,
           kv_mod_w, kv_mod_b, kv_norm_g, w_kv, attn_w_q, attn_w_o, rel_bias):
    bsz, s, d = x.shape
    depth = mod_w.shape[0]
    n_a = conv_w_in.shape[0]
    n_heads = rel_bias.shape[1]
    assert d // n_heads == HEAD_DIM and s % ROW_TILE == 0

    mod = _modvec(c, mod_w, mod_b.reshape(depth, 1, 6 * d)).reshape(depth, bsz, 6, d)
    kvmod = _modvec(c, kv_mod_w[None], kv_mod_b.reshape(1, 1, 2 * d)).reshape(1, bsz, 2, d)

    conv_w_in, conv_w_out = conv_w_in.astype(BF16), conv_w_out.astype(BF16)
    kv_norm_g = kv_norm_g.reshape(1, d)
    w_kv = w_kv[None]

    k = v = None
    attn_w = None
    for layer in range(depth):
        ffn_casts = [(ffn_w_in, layer), (ffn_w_out, layer)]
        if layer < n_a:
            x, (w_in, w_out) = _mixer(x, mod, norm_g, layer, conv_w_in, conv_k, conv_w_out,
                                      layer, ffn_casts)
        else:
            bi = layer - n_a
            if attn_w is None:
                attn_w = (w_kv[0].astype(BF16), attn_w_q[bi].astype(BF16), attn_w_o[bi].astype(BF16))
            w_kv_b, w_q_b, w_o_b = attn_w
            (k_new, v_new, q), (w_in, w_out) = _proj(x, mod, kvmod, norm_g, layer, kv_norm_g,
                                                     w_kv_b, w_q_b, ffn_casts)
            if layer == n_a:
                k, v = k_new, v_new
            bias = _bias_table(rel_bias, bi).reshape(2, n_heads // 2, 2 * CHUNK, SLAB)
            x = _attn(x, q, k, v, bias, w_o_b, mod, norm_g, layer)
        nxt = layer + 1
        attn_casts = []
        if max(n_a, 1) <= nxt < depth:
            attn_casts = [(w_kv, 0), (attn_w_q, nxt - n_a), (attn_w_o, nxt - n_a)]
        x, attn_w = _ffn(x, mod, norm_g, layer, w_in, w_out, attn_casts)
        attn_w = tuple(attn_w) if attn_casts else None
    return x
```

```python
import functools
import math

import jax
import jax.numpy as jnp
from jax import lax
from jax.experimental import pallas as pl
from jax.experimental.pallas import tpu as pltpu

EPS = 1e-6
CHUNK = 64
N_LEFT_CHUNKS = 8
BAND = (N_LEFT_CHUNKS + 1) * CHUNK
SLAB = BAND + CHUNK
GROUP = 4
WIN = (GROUP + N_LEFT_CHUNKS) * CHUNK
HEAD_DIM = 64
MAX_REL = 2 * CHUNK
LOG2_E = math.log2(math.e)

V7X_LANES = 128
V7X_BF16_SUBLANES = 16
ROW_TILE = 512
V7X_VMEM_LIMIT = 56 * 1024 * 1024

BF16 = jnp.bfloat16
F32 = jnp.float32


def _dot(a, b):
    return jnp.dot(a, b, preferred_element_type=F32)


def _rms(x, g):
    return x * lax.rsqrt(jnp.mean(x * x, axis=-1, keepdims=True) + EPS) * g


def _resident(block_shape, index_map):
    return pl.BlockSpec(block_shape, index_map, pipeline_mode=pl.Buffered(1))


def _params(n_grid_dims):
    return pltpu.CompilerParams(
        dimension_semantics=("arbitrary",) * n_grid_dims,
        vmem_limit_bytes=V7X_VMEM_LIMIT)


def _with_casts(body, n_in, n_out, n_cast):
    def wrapped(*refs):
        srcs = refs[n_in:n_in + n_cast]
        outs_end = n_in + n_cast + n_out
        dsts = refs[outs_end:outs_end + n_cast]
        for src, dst in zip(srcs, dsts):
            dst[...] = src[...].astype(BF16)
        body(*refs[:n_in], *refs[n_in + n_cast:outs_end], *refs[outs_end + n_cast:])
    return wrapped


def _tiled_call(body, name, grid, in_specs, args, out_specs, out_shapes, scratch=(), casts=()):
    bsz, n_tiles = grid
    c_in, c_out, c_shapes, c_args = [], [], [], []
    for w, idx in casts:
        r, c = w.shape[1:]
        rows, cols = r // n_tiles, c // bsz
        assert rows * n_tiles == r and rows % V7X_BF16_SUBLANES == 0
        assert cols * bsz == c and cols % V7X_LANES == 0
        c_in.append(pl.BlockSpec((None, rows, cols), lambda b, i, idx=idx: (idx, i, b)))
        c_out.append(pl.BlockSpec((rows, cols), lambda b, i: (i, b)))
        c_shapes.append(jax.ShapeDtypeStruct((r, c), BF16))
        c_args.append(w)
    res = pl.pallas_call(
        _with_casts(body, len(in_specs), len(out_specs), len(casts)),
        grid=grid,
        in_specs=list(in_specs) + c_in,
        out_specs=list(out_specs) + c_out,
        out_shape=list(out_shapes) + c_shapes,
        scratch_shapes=list(scratch),
        compiler_params=_params(len(grid)),
        name=name,
    )(*args, *c_args)
    return res[:len(out_specs)], res[len(out_specs):]


def _modvec_kernel(c_ref, w_ref, b_ref, o_ref):
    s = jax.nn.silu(c_ref[...]).astype(BF16)
    o_ref[...] = _dot(s, w_ref[...].astype(BF16)) + b_ref[...]


def _modvec(c, w, b, col_tile=1024):
    n_layers, d, n = w.shape
    bsz = c.shape[0]
    return pl.pallas_call(
        _modvec_kernel,
        grid=(n_layers, n // col_tile),
        in_specs=[
            pl.BlockSpec((bsz, d), lambda l, j: (0, 0)),
            pl.BlockSpec((None, d, col_tile), lambda l, j: (l, 0, j)),
            pl.BlockSpec((None, 1, col_tile), lambda l, j: (l, 0, j)),
        ],
        out_specs=pl.BlockSpec((None, bsz, col_tile), lambda l, j: (l, 0, j)),
        out_shape=jax.ShapeDtypeStruct((n_layers, bsz, n), F32),
        compiler_params=_params(2),
        name="modvec",
    )(c, w, b)


def _mixer_kernel(x_ref, mod_ref, ng_ref, win_ref, ck_ref, wout_ref, o_ref, zbuf):
    tm, d = x_ref.shape
    x = x_ref[...]
    sh1, sc1, g1 = mod_ref[0:1, :], mod_ref[1:2, :], mod_ref[2:3, :]
    h = (_rms(x, ng_ref[0:1, :]) * (1.0 + sc1) + sh1).astype(BF16)
    b_gate = _dot(h, win_ref[:, 0:d])
    z = _dot(h, win_ref[:, d:2 * d]) * _dot(h, win_ref[:, 2 * d:3 * d])

    @pl.when(pl.program_id(1) == 0)
    def _():
        zbuf[0:8, :] = jnp.zeros((8, d), F32)

    zbuf[8:8 + tm, :] = z
    conv = (ck_ref[0:1, :] * zbuf[6:6 + tm, :]
            + ck_ref[1:2, :] * zbuf[7:7 + tm, :]
            + ck_ref[2:3, :] * z)
    zbuf[0:8, :] = zbuf[tm:tm + 8, :]
    y = _dot((b_gate * conv).astype(BF16), wout_ref[...])
    o_ref[...] = x + g1 * _rms(y, ng_ref[1:2, :])


def _mixer(x, mod, norm_g, layer, w_in, conv_k, w_out, a_idx, casts):
    bsz, s, d = x.shape
    tm = ROW_TILE
    row_spec = pl.BlockSpec((None, tm, d), lambda b, i: (b, i, 0))
    (out,), cast = _tiled_call(
        _mixer_kernel, "conv_mixer", (bsz, s // tm),
        in_specs=[
            row_spec,
            pl.BlockSpec((None, None, 6, d), lambda b, i: (layer, b, 0, 0)),
            _resident((None, 4, d), lambda b, i: (layer, 0, 0)),
            _resident((None, d, 3 * d), lambda b, i: (a_idx, 0, 0)),
            _resident((None, 3, d), lambda b, i: (a_idx, 0, 0)),
            _resident((None, d, d), lambda b, i: (a_idx, 0, 0)),
        ],
        args=(x, mod, norm_g, w_in, conv_k, w_out),
        out_specs=[row_spec],
        out_shapes=[jax.ShapeDtypeStruct(x.shape, F32)],
        scratch=[pltpu.VMEM((tm + 8, d), F32)],
        casts=casts)
    return out, cast


def _ffn_kernel(x_ref, mod_ref, ng_ref, win_ref, wout_ref, o_ref, *, n_sub):
    sh2, sc2, g2 = mod_ref[3:4, :], mod_ref[4:5, :], mod_ref[5:6, :]
    f = wout_ref.shape[0]
    rows = x_ref.shape[0] // n_sub
    for r in range(n_sub):
        rs = slice(r * rows, (r + 1) * rows)
        x = x_ref[rs, :]
        h = (_rms(x, ng_ref[2:3, :]) * (1.0 + sc2) + sh2).astype(BF16)
        g = _dot(h, win_ref[:, 0:f])
        u = _dot(h, win_ref[:, f:2 * f])
        y = _dot((jax.nn.silu(g) * u).astype(BF16), wout_ref[...])
        o_ref[rs, :] = x + g2 * _rms(y, ng_ref[3:4, :])


def _ffn(x, mod, norm_g, layer, w_in, w_out, casts):
    bsz, s, d = x.shape
    f = w_out.shape[0]
    tm = ROW_TILE
    row_spec = pl.BlockSpec((None, tm, d), lambda b, i: (b, i, 0))
    (out,), cast = _tiled_call(
        functools.partial(_ffn_kernel, n_sub=2), "swiglu_ffn", (bsz, s // tm),
        in_specs=[
            row_spec,
            pl.BlockSpec((None, None, 6, d), lambda b, i: (layer, b, 0, 0)),
            _resident((None, 4, d), lambda b, i: (layer, 0, 0)),
            _resident((d, 2 * f), lambda b, i: (0, 0)),
            _resident((f, d), lambda b, i: (0, 0)),
        ],
        args=(x, mod, norm_g, w_in, w_out),
        out_specs=[row_spec],
        out_shapes=[jax.ShapeDtypeStruct(x.shape, F32)],
        casts=casts)
    return out, cast


def _proj_kernel(x_ref, mod_ref, kvmod_ref, ng_ref, kvg_ref, wkv_ref, wq_ref,
                 k_ref, v_ref, q_ref):
    d = x_ref.shape[1]
    x = x_ref[...]
    xn = x * lax.rsqrt(jnp.mean(x * x, axis=-1, keepdims=True) + EPS)
    hkv = ((xn * kvg_ref[...]) * (1.0 + kvmod_ref[1:2, :]) + kvmod_ref[0:1, :]).astype(BF16)
    k_ref[...] = _dot(hkv, wkv_ref[:, 0:d]).astype(BF16)
    v_ref[...] = _dot(hkv, wkv_ref[:, d:2 * d]).astype(BF16)
    h = ((xn * ng_ref[0:1, :]) * (1.0 + mod_ref[1:2, :]) + mod_ref[0:1, :]).astype(BF16)
    q_ref[...] = (_dot(h, wq_ref[...]) * (LOG2_E * HEAD_DIM ** -0.5)).astype(BF16)


def _proj(x, mod, kvmod, norm_g, layer, kv_norm_g, w_kv, w_q, casts):
    bsz, s, d = x.shape
    tm = ROW_TILE
    row_spec = pl.BlockSpec((None, tm, d), lambda b, i: (b, i, 0))
    out = jax.ShapeDtypeStruct(x.shape, BF16)
    return _tiled_call(
        _proj_kernel, "qkv_proj", (bsz, s // tm),
        in_specs=[
            row_spec,
            pl.BlockSpec((None, None, 6, d), lambda b, i: (layer, b, 0, 0)),
            pl.BlockSpec((None, None, 2, d), lambda b, i: (0, b, 0, 0)),
            _resident((None, 4, d), lambda b, i: (layer, 0, 0)),
            _resident((1, d), lambda b, i: (0, 0)),
            _resident((d, 2 * d), lambda b, i: (0, 0)),
            _resident((d, d), lambda b, i: (0, 0)),
        ],
        args=(x, mod, kvmod, norm_g, kv_norm_g, w_kv, w_q),
        out_specs=[row_spec, row_spec, row_spec],
        out_shapes=[out, out, out],
        casts=casts)


def _bias_kernel(rb_ref, o_ref):
    n_heads = rb_ref.shape[0]
    n_tab = 2 * MAX_REL
    rb = rb_ref[:, 0:n_tab]
    hi = rb.astype(BF16)
    r1 = rb - hi.astype(F32)
    mid = r1.astype(BF16)
    lo = (r1 - mid.astype(F32)).astype(BF16)
    pieces = jnp.concatenate([hi, mid, lo], axis=0)
    rb_top = rb_ref[:, n_tab:n_tab + 1]
    u = lax.broadcasted_iota(jnp.int32, (1, WIN), 1)
    m = jnp.where(u >= WIN - CHUNK, u - WIN, u)
    idx = jnp.clip(N_LEFT_CHUNKS * CHUNK - m, -MAX_REL, MAX_REL) + MAX_REL
    row = lax.broadcasted_iota(jnp.int32, (n_tab, WIN), 0)
    onehot = jnp.where(row == idx, 1.0, 0.0).astype(BF16)
    t3 = _dot(pieces, onehot)
    w = t3[0:n_heads] + t3[n_heads:2 * n_heads] + t3[2 * n_heads:3 * n_heads]
    w = (w + jnp.where(idx == n_tab, rb_top, 0.0)) * LOG2_E
    col = lax.broadcasted_iota(jnp.int32, (CHUNK, WIN), 1)
    neg = jnp.finfo(F32).min
    for var in range(2):
        kk = col - var * CHUNK
        in_band = (kk >= 0) & (kk < BAND)
        for h in range(n_heads):
            t = pltpu.roll(jnp.broadcast_to(w[h:h + 1, :], (CHUNK, WIN)), var * CHUNK, 1,
                           stride=1, stride_axis=0)
            o_ref[var, h] = jnp.where(in_band, t, neg)[:, 0:SLAB]


def _bias_table(rel_bias, b_idx):
    n_heads, n_rel = rel_bias.shape[1:]
    return pl.pallas_call(
        _bias_kernel,
        grid=(1,),
        in_specs=[pl.BlockSpec((None, n_heads, n_rel), lambda i: (b_idx, 0, 0))],
        out_specs=pl.BlockSpec((2, n_heads, CHUNK, SLAB), lambda i: (0, 0, 0, 0)),
        out_shape=jax.ShapeDtypeStruct((2, n_heads, CHUNK, SLAB), F32),
        compiler_params=_params(1),
        name="rel_bias_table",
    )(rel_bias)


def _attn_kernel(*refs, first_tile):
    if first_tile:
        (x_ref, q_ref, k_ref, v_ref, bias_ref, wo_ref, mod_ref, ng_ref, o_ref,
         obuf, q2buf, sbuf, pbuf, kwin, vwin) = refs
        tm, d = q_ref.shape
        for src, win in ((k_ref, kwin), (v_ref, vwin)):
            win[0:tm, :] = jnp.zeros((tm, d), BF16)
            win[tm:2 * tm, :] = src[...]
    else:
        (x_ref, q_ref, kwin, vwin, bias_ref, wo_ref, mod_ref, ng_ref, _, o_ref,
         obuf, q2buf, sbuf, pbuf) = refs
        kwin, vwin = kwin.at[0], vwin.at[0]
        tm, d = q_ref.shape
    low_head = lax.broadcasted_iota(jnp.int32, (CHUNK, V7X_LANES), 1) < HEAD_DIM
    neg = jnp.finfo(F32).min
    ones = jnp.ones((WIN, V7X_LANES), BF16)
    half = GROUP * CHUNK
    for slot in range(2):
        pbuf[slot, 0:half, SLAB:WIN] = jnp.zeros((half, WIN - SLAB), BF16)
        pbuf[slot, half:2 * half, 0:V7X_LANES] = jnp.zeros((half, V7X_LANES), BF16)
    body = 0
    for g in range(tm // (GROUP * CHUNK)):
        w0 = g * GROUP * CHUNK
        for p in range(d // V7X_LANES):
            ls = slice(p * V7X_LANES, (p + 1) * V7X_LANES)
            slot = body % 2
            body += 1
            for j in range(GROUP):
                qp = q_ref[w0 + j * CHUNK:w0 + (j + 1) * CHUNK, ls]
                zero = jnp.zeros_like(qp)
                q2buf[slot, (2 * j) * CHUNK:(2 * j + 1) * CHUNK, :] = jnp.where(low_head, qp, zero)
                q2buf[slot, (2 * j + 1) * CHUNK:(2 * j + 2) * CHUNK, :] = jnp.where(low_head, zero, qp)
            sbuf[slot] = lax.dot_general(q2buf[slot], kwin[w0:w0 + WIN, ls],
                                         (((1,), (1,)), ((), ())), preferred_element_type=F32)
            if first_tile:
                sbuf[slot, :, 0:tm - w0] = jnp.full((2 * half, tm - w0), neg, F32)
            for j in range(GROUP):
                t0 = (j // 2) * V7X_LANES
                for e in range(2):
                    rows = slice((2 * j + e) * CHUNK, (2 * j + e + 1) * CHUNK)
                    s = sbuf[slot, rows, t0:t0 + SLAB] + bias_ref[j % 2, p, e * CHUNK:(e + 1) * CHUNK, :]
                    ex = jnp.exp2(s - jnp.max(s, axis=-1, keepdims=True))
                    pbuf[slot, rows, t0:t0 + SLAB] = ex.astype(BF16)
            v_ones = jnp.concatenate([vwin[w0:w0 + WIN, ls], ones], axis=1)
            o2 = _dot(pbuf[slot], v_ones)
            o2 = o2[:, 0:V7X_LANES] / o2[:, V7X_LANES:2 * V7X_LANES]
            for j in range(GROUP):
                lo = o2[(2 * j) * CHUNK:(2 * j + 1) * CHUNK, :]
                hi = o2[(2 * j + 1) * CHUNK:(2 * j + 2) * CHUNK, :]
                obuf[w0 + j * CHUNK:w0 + (j + 1) * CHUNK, ls] = jnp.where(low_head, lo, hi).astype(BF16)
    y = _dot(obuf[...], wo_ref[...])
    o_ref[...] = x_ref[...] + mod_ref[2:3, :] * _rms(y, ng_ref[1:2, :])


def _attn(x, q, k, v, bias, w_o, mod, norm_g, layer):
    bsz, s, d = x.shape
    tm = ROW_TILE
    n_tiles = s // tm
    assert tm == N_LEFT_CHUNKS * CHUNK and tm % (GROUP * CHUNK) == 0
    n_pairs = d // V7X_LANES
    m_rows = 2 * GROUP * CHUNK
    scratch = [pltpu.VMEM((tm, d), BF16),
               pltpu.VMEM((2, m_rows, V7X_LANES), BF16),
               pltpu.VMEM((2, m_rows, WIN), F32),
               pltpu.VMEM((2, m_rows, WIN), BF16)]

    def call(first_tile, out_so_far=None):
        t_off = 0 if first_tile else 1
        row_spec = pl.BlockSpec((None, tm, d), lambda b, i: (b, i + t_off, 0))
        if first_tile:
            kv_spec = row_spec
        else:
            kv_spec = pl.BlockSpec((pl.Element(1), pl.Element(2 * tm), pl.Element(d)),
                                   lambda b, i: (b, i * tm, 0))
        in_specs = [
            row_spec, row_spec, kv_spec, kv_spec,
            _resident((2, n_pairs, 2 * CHUNK, SLAB), lambda b, i: (0, 0, 0, 0)),
            _resident((d, d), lambda b, i: (0, 0)),
            pl.BlockSpec((None, None, 6, d), lambda b, i: (layer, b, 0, 0)),
            _resident((None, 4, d), lambda b, i: (layer, 0, 0)),
        ]
        args = [x, q, k, v, bias, w_o, mod, norm_g]
        if first_tile:
            kv_scratch = [pltpu.VMEM((2 * tm, d), BF16), pltpu.VMEM((2 * tm, d), BF16)]
            aliases = {}
        else:
            in_specs.append(pl.BlockSpec(memory_space=pl.ANY))
            args.append(out_so_far)
            kv_scratch = []
            aliases = {len(args) - 1: 0}
        return pl.pallas_call(
            functools.partial(_attn_kernel, first_tile=first_tile),
            grid=(bsz, 1 if first_tile else n_tiles - 1),
            in_specs=in_specs,
            out_specs=row_spec,
            out_shape=jax.ShapeDtypeStruct(x.shape, F32),
            scratch_shapes=scratch + kv_scratch,
            input_output_aliases=aliases,
            compiler_params=_params(2),
            name="band_attention_first" if first_tile else "band_attention",
        )(*args)

    return call(False, call(True))


def kernel(x, c, mod_w, mod_b, norm_g, ffn_w_in, ffn_w_out, conv_w_in, conv_k, conv_w_out,
           kv_mod_w, kv_mod_b, kv_norm_g, w_kv, attn_w_q, attn_w_o, rel_bias):
    bsz, s, d = x.shape
    depth = mod_w.shape[0]
    n_a = conv_w_in.shape[0]
    n_heads = rel_bias.shape[1]
    assert d // n_heads == HEAD_DIM and s % ROW_TILE == 0

    mod = _modvec(c, mod_w, mod_b.reshape(depth, 1, 6 * d)).reshape(depth, bsz, 6, d)
    kvmod = _modvec(c, kv_mod_w[None], kv_mod_b.reshape(1, 1, 2 * d)).reshape(1, bsz, 2, d)

    conv_w_in, conv_w_out = conv_w_in.astype(BF16), conv_w_out.astype(BF16)
    kv_norm_g = kv_norm_g.reshape(1, d)
    w_kv = w_kv[None]

    k = v = None
    attn_w = None
    for layer in range(depth):
        ffn_casts = [(ffn_w_in, layer), (ffn_w_out, layer)]
        if layer < n_a:
            x, (w_in, w_out) = _mixer(x, mod, norm_g, layer, conv_w_in, conv_k, conv_w_out,
                                      layer, ffn_casts)
        else:
            bi = layer - n_a
            if attn_w is None:
                attn_w = (w_kv[0].astype(BF16), attn_w_q[bi].astype(BF16), attn_w_o[bi].astype(BF16))
            w_kv_b, w_q_b, w_o_b = attn_w
            (k_new, v_new, q), (w_in, w_out) = _proj(x, mod, kvmod, norm_g, layer, kv_norm_g,
                                                     w_kv_b, w_q_b, ffn_casts)
            if layer == n_a:
                k, v = k_new, v_new
            bias = _bias_table(rel_bias, bi).reshape(2, n_heads // 2, 2 * CHUNK, SLAB)
            x = _attn(x, q, k, v, bias, w_o_b, mod, norm_g, layer)
        nxt = layer + 1
        attn_casts = []
        if max(n_a, 1) <= nxt < depth:
            attn_casts = [(w_kv, 0), (attn_w_q, nxt - n_a), (attn_w_o, nxt - n_a)]
        x, attn_w = _ffn(x, mod, norm_g, layer, w_in, w_out, attn_casts)
        attn_w = tuple(attn_w) if attn_casts else None
    return x
```

```python
import functools
import math

import jax
import jax.numpy as jnp
from jax import lax
from jax.experimental import pallas as pl
from jax.experimental.pallas import tpu as pltpu

EPS = 1e-6
CHUNK = 64
N_LEFT_CHUNKS = 8
BAND = (N_LEFT_CHUNKS + 1) * CHUNK
SLAB = BAND + CHUNK
GROUP = 4
WIN = (GROUP + N_LEFT_CHUNKS) * CHUNK
HEAD_DIM = 64
MAX_REL = 2 * CHUNK
LOG2_E = math.log2(math.e)

V7X_LANES = 128
V7X_BF16_SUBLANES = 16
V7X_MXU_COLS = 256
ROW_TILE = 512
V7X_VMEM_LIMIT = 56 * 1024 * 1024

BF16 = jnp.bfloat16
F32 = jnp.float32


def _dot(a, b):
    return jnp.dot(a, b, preferred_element_type=F32)


def _unit_rms(x):
    return x * lax.rsqrt(jnp.mean(x * x, axis=-1, keepdims=True) + EPS)


def _norm_modulate(xn, g, scale, shift):
    return xn * (g * (1.0 + scale)) + shift


def _gated_norm(y, g, gate):
    return _unit_rms(y) * (gate * g)


def _resident(block_shape, index_map):
    return pl.BlockSpec(block_shape, index_map, pipeline_mode=pl.Buffered(1))


def _params(n_grid_dims):
    return pltpu.CompilerParams(
        dimension_semantics=("arbitrary",) * n_grid_dims,
        vmem_limit_bytes=V7X_VMEM_LIMIT)


def _with_casts(body, n_in, n_out, n_cast):
    def wrapped(*refs):
        srcs = refs[n_in:n_in + n_cast]
        outs_end = n_in + n_cast + n_out
        dsts = refs[outs_end:outs_end + n_cast]
        for src, dst in zip(srcs, dsts):
            dst[...] = src[...].astype(BF16)
        body(*refs[:n_in], *refs[n_in + n_cast:outs_end], *refs[outs_end + n_cast:])
    return wrapped


def _tiled_call(body, name, grid, in_specs, args, out_specs, out_shapes, scratch=(), casts=()):
    bsz, n_tiles = grid
    c_in, c_out, c_shapes, c_args = [], [], [], []
    for w, idx in casts:
        r, c = w.shape[1:]
        rows, cols = r // n_tiles, c // bsz
        assert rows * n_tiles == r and rows % V7X_BF16_SUBLANES == 0
        assert cols * bsz == c and cols % V7X_LANES == 0
        c_in.append(pl.BlockSpec((None, rows, cols), lambda b, i, idx=idx: (idx, i, b)))
        c_out.append(pl.BlockSpec((rows, cols), lambda b, i: (i, b)))
        c_shapes.append(jax.ShapeDtypeStruct((r, c), BF16))
        c_args.append(w)
    res = pl.pallas_call(
        _with_casts(body, len(in_specs), len(out_specs), len(casts)),
        grid=grid,
        in_specs=list(in_specs) + c_in,
        out_specs=list(out_specs) + c_out,
        out_shape=list(out_shapes) + c_shapes,
        scratch_shapes=list(scratch),
        compiler_params=_params(len(grid)),
        name=name,
    )(*args, *c_args)
    return res[:len(out_specs)], res[len(out_specs):]


def _modvec_kernel(c_ref, w_ref, b_ref, o_ref):
    s = jax.nn.silu(c_ref[...]).astype(BF16)
    o_ref[...] = _dot(s, w_ref[...].astype(BF16)) + b_ref[...]


def _modvec(c, w, b, col_tile=1024):
    n_layers, d, n = w.shape
    bsz = c.shape[0]
    return pl.pallas_call(
        _modvec_kernel,
        grid=(n_layers, n // col_tile),
        in_specs=[
            pl.BlockSpec((bsz, d), lambda l, j: (0, 0)),
            pl.BlockSpec((None, d, col_tile), lambda l, j: (l, 0, j)),
            pl.BlockSpec((None, 1, col_tile), lambda l, j: (l, 0, j)),
        ],
        out_specs=pl.BlockSpec((None, bsz, col_tile), lambda l, j: (l, 0, j)),
        out_shape=jax.ShapeDtypeStruct((n_layers, bsz, n), F32),
        compiler_params=_params(2),
        name="modvec",
    )(c, w, b)


def _mixer_kernel(x_ref, mod_ref, ng_ref, win_ref, ck_ref, wout_ref, o_ref, zbuf):
    tm, d = x_ref.shape
    x = x_ref[...]
    sh1, sc1, g1 = mod_ref[0:1, :], mod_ref[1:2, :], mod_ref[2:3, :]
    h = _norm_modulate(_unit_rms(x), ng_ref[0:1, :], sc1, sh1).astype(BF16)

    @pl.when(pl.program_id(1) == 0)
    def _():
        zbuf[0:8, :] = jnp.zeros((8, d), F32)

    us = []
    for c0 in range(0, d, V7X_MXU_COLS):
        cs = slice(c0, c0 + V7X_MXU_COLS)
        z = (_dot(h, win_ref[:, d + c0:d + c0 + V7X_MXU_COLS])
             * _dot(h, win_ref[:, 2 * d + c0:2 * d + c0 + V7X_MXU_COLS]))
        zbuf[8:8 + tm, cs] = z
        conv = (ck_ref[0:1, cs] * zbuf[6:6 + tm, cs]
                + ck_ref[1:2, cs] * zbuf[7:7 + tm, cs]
                + ck_ref[2:3, cs] * z)
        us.append((_dot(h, win_ref[:, cs]) * conv).astype(BF16))
    zbuf[0:8, :] = zbuf[tm:tm + 8, :]
    y = _dot(jnp.concatenate(us, axis=1), wout_ref[...])
    o_ref[...] = x + _gated_norm(y, ng_ref[1:2, :], g1)


def _mixer(x, mod, norm_g, layer, w_in, conv_k, w_out, a_idx, casts):
    bsz, s, d = x.shape
    tm = ROW_TILE
    row_spec = pl.BlockSpec((None, tm, d), lambda b, i: (b, i, 0))
    (out,), cast = _tiled_call(
        _mixer_kernel, "conv_mixer", (bsz, s // tm),
        in_specs=[
            row_spec,
            pl.BlockSpec((None, None, 6, d), lambda b, i: (layer, b, 0, 0)),
            _resident((None, 4, d), lambda b, i: (layer, 0, 0)),
            _resident((None, d, 3 * d), lambda b, i: (a_idx, 0, 0)),
            _resident((None, 3, d), lambda b, i: (a_idx, 0, 0)),
            _resident((None, d, d), lambda b, i: (a_idx, 0, 0)),
        ],
        args=(x, mod, norm_g, w_in, conv_k, w_out),
        out_specs=[row_spec],
        out_shapes=[jax.ShapeDtypeStruct(x.shape, F32)],
        scratch=[pltpu.VMEM((tm + 8, d), F32)],
        casts=casts)
    return out, cast


def _ffn_kernel(x_ref, mod_ref, ng_ref, win_ref, wout_ref, o_ref, *, n_sub):
    sh2, sc2, g2 = mod_ref[3:4, :], mod_ref[4:5, :], mod_ref[5:6, :]
    f = wout_ref.shape[0]
    rows = x_ref.shape[0] // n_sub
    for r in range(n_sub):
        rs = slice(r * rows, (r + 1) * rows)
        x = x_ref[rs, :]
        h = _norm_modulate(_unit_rms(x), ng_ref[2:3, :], sc2, sh2).astype(BF16)
        g = _dot(h, win_ref[:, 0:f])
        u = _dot(h, win_ref[:, f:2 * f])
        y = _dot((jax.nn.silu(g) * u).astype(BF16), wout_ref[...])
        o_ref[rs, :] = x + _gated_norm(y, ng_ref[3:4, :], g2)


def _ffn(x, mod, norm_g, layer, w_in, w_out, casts):
    bsz, s, d = x.shape
    f = w_out.shape[0]
    tm = ROW_TILE
    row_spec = pl.BlockSpec((None, tm, d), lambda b, i: (b, i, 0))
    (out,), cast = _tiled_call(
        functools.partial(_ffn_kernel, n_sub=2), "swiglu_ffn", (bsz, s // tm),
        in_specs=[
            row_spec,
            pl.BlockSpec((None, None, 6, d), lambda b, i: (layer, b, 0, 0)),
            _resident((None, 4, d), lambda b, i: (layer, 0, 0)),
            _resident((d, 2 * f), lambda b, i: (0, 0)),
            _resident((f, d), lambda b, i: (0, 0)),
        ],
        args=(x, mod, norm_g, w_in, w_out),
        out_specs=[row_spec],
        out_shapes=[jax.ShapeDtypeStruct(x.shape, F32)],
        casts=casts)
    return out, cast


def _proj_kernel(x_ref, mod_ref, kvmod_ref, ng_ref, kvg_ref, wkv_ref, wq_ref,
                 k_ref, v_ref, q_ref):
    d = x_ref.shape[1]
    x = x_ref[...]
    xn = _unit_rms(x)
    hkv = _norm_modulate(xn, kvg_ref[...], kvmod_ref[1:2, :], kvmod_ref[0:1, :]).astype(BF16)
    k_ref[...] = _dot(hkv, wkv_ref[:, 0:d]).astype(BF16)
    v_ref[...] = _dot(hkv, wkv_ref[:, d:2 * d]).astype(BF16)
    h = _norm_modulate(xn, ng_ref[0:1, :], mod_ref[1:2, :], mod_ref[0:1, :]).astype(BF16)
    q_ref[...] = (_dot(h, wq_ref[...]) * (LOG2_E * HEAD_DIM ** -0.5)).astype(BF16)


def _proj(x, mod, kvmod, norm_g, layer, kv_norm_g, w_kv, w_q, casts):
    bsz, s, d = x.shape
    tm = ROW_TILE
    row_spec = pl.BlockSpec((None, tm, d), lambda b, i: (b, i, 0))
    out = jax.ShapeDtypeStruct(x.shape, BF16)
    return _tiled_call(
        _proj_kernel, "qkv_proj", (bsz, s // tm),
        in_specs=[
            row_spec,
            pl.BlockSpec((None, None, 6, d), lambda b, i: (layer, b, 0, 0)),
            pl.BlockSpec((None, None, 2, d), lambda b, i: (0, b, 0, 0)),
            _resident((None, 4, d), lambda b, i: (layer, 0, 0)),
            _resident((1, d), lambda b, i: (0, 0)),
            _resident((d, 2 * d), lambda b, i: (0, 0)),
            _resident((d, d), lambda b, i: (0, 0)),
        ],
        args=(x, mod, kvmod, norm_g, kv_norm_g, w_kv, w_q),
        out_specs=[row_spec, row_spec, row_spec],
        out_shapes=[out, out, out],
        casts=casts)


def _bias_kernel(rb_ref, o_ref):
    n_heads = rb_ref.shape[0]
    n_tab = 2 * MAX_REL
    rb = rb_ref[:, 0:n_tab]
    hi = rb.astype(BF16)
    r1 = rb - hi.astype(F32)
    mid = r1.astype(BF16)
    lo = (r1 - mid.astype(F32)).astype(BF16)
    pieces = jnp.concatenate([hi, mid, lo], axis=0)
    rb_top = rb_ref[:, n_tab:n_tab + 1]
    u = lax.broadcasted_iota(jnp.int32, (1, WIN), 1)
    m = jnp.where(u >= WIN - CHUNK, u - WIN, u)
    idx = jnp.clip(N_LEFT_CHUNKS * CHUNK - m, -MAX_REL, MAX_REL) + MAX_REL
    row = lax.broadcasted_iota(jnp.int32, (n_tab, WIN), 0)
    onehot = jnp.where(row == idx, 1.0, 0.0).astype(BF16)
    t3 = _dot(pieces, onehot)
    w = t3[0:n_heads] + t3[n_heads:2 * n_heads] + t3[2 * n_heads:3 * n_heads]
    w = (w + jnp.where(idx == n_tab, rb_top, 0.0)) * LOG2_E
    col = lax.broadcasted_iota(jnp.int32, (CHUNK, WIN), 1)
    neg = jnp.finfo(F32).min
    for var in range(2):
        kk = col - var * CHUNK
        in_band = (kk >= 0) & (kk < BAND)
        for h in range(n_heads):
            t = pltpu.roll(jnp.broadcast_to(w[h:h + 1, :], (CHUNK, WIN)), var * CHUNK, 1,
                           stride=1, stride_axis=0)
            o_ref[var, h] = jnp.where(in_band, t, neg)[:, 0:SLAB]


def _bias_table(rel_bias, b_idx):
    n_heads, n_rel = rel_bias.shape[1:]
    return pl.pallas_call(
        _bias_kernel,
        grid=(1,),
        in_specs=[pl.BlockSpec((None, n_heads, n_rel), lambda i: (b_idx, 0, 0))],
        out_specs=pl.BlockSpec((2, n_heads, CHUNK, SLAB), lambda i: (0, 0, 0, 0)),
        out_shape=jax.ShapeDtypeStruct((2, n_heads, CHUNK, SLAB), F32),
        compiler_params=_params(1),
        name="rel_bias_table",
    )(rel_bias)


def _attn_tile(x_ref, q_ref, kwin, vwin, bias_ref, wo_ref, mod_ref, ng_ref, o_ref,
               obuf, q2buf, sbuf, pbuf, *, first_tile):
    tm, d = q_ref.shape
    low_head = lax.broadcasted_iota(jnp.int32, (CHUNK, V7X_LANES), 1) < HEAD_DIM
    neg = jnp.finfo(F32).min
    ones = jnp.ones((WIN, V7X_LANES), BF16)
    half = GROUP * CHUNK
    for slot in range(2):
        pbuf[slot, 0:half, SLAB:WIN] = jnp.zeros((half, WIN - SLAB), BF16)
        pbuf[slot, half:2 * half, 0:V7X_LANES] = jnp.zeros((half, V7X_LANES), BF16)
    body = 0
    for g in range(tm // (GROUP * CHUNK)):
        w0 = g * GROUP * CHUNK
        for p in range(d // V7X_LANES):
            ls = slice(p * V7X_LANES, (p + 1) * V7X_LANES)
            slot = body % 2
            body += 1
            for j in range(GROUP):
                qp = q_ref[w0 + j * CHUNK:w0 + (j + 1) * CHUNK, ls]
                zero = jnp.zeros_like(qp)
                q2buf[slot, (2 * j) * CHUNK:(2 * j + 1) * CHUNK, :] = jnp.where(low_head, qp, zero)
                q2buf[slot, (2 * j + 1) * CHUNK:(2 * j + 2) * CHUNK, :] = jnp.where(low_head, zero, qp)
            sbuf[slot] = lax.dot_general(q2buf[slot], kwin[w0:w0 + WIN, ls],
                                         (((1,), (1,)), ((), ())), preferred_element_type=F32)
            if first_tile:
                sbuf[slot, :, 0:tm - w0] = jnp.full((2 * half, tm - w0), neg, F32)
            for j in range(GROUP):
                t0 = (j // 2) * V7X_LANES
                for e in range(2):
                    rows = slice((2 * j + e) * CHUNK, (2 * j + e + 1) * CHUNK)
                    s = sbuf[slot, rows, t0:t0 + SLAB] + bias_ref[j % 2, p, e * CHUNK:(e + 1) * CHUNK, :]
                    ex = jnp.exp2(s - jnp.max(s, axis=-1, keepdims=True))
                    pbuf[slot, rows, t0:t0 + SLAB] = ex.astype(BF16)
            v_ones = jnp.concatenate([vwin[w0:w0 + WIN, ls], ones], axis=1)
            o2 = _dot(pbuf[slot], v_ones)
            o2 = o2[:, 0:V7X_LANES] / o2[:, V7X_LANES:2 * V7X_LANES]
            for j in range(GROUP):
                lo = o2[(2 * j) * CHUNK:(2 * j + 1) * CHUNK, :]
                hi = o2[(2 * j + 1) * CHUNK:(2 * j + 2) * CHUNK, :]
                obuf[w0 + j * CHUNK:w0 + (j + 1) * CHUNK, ls] = jnp.where(low_head, lo, hi).astype(BF16)
    y = _dot(obuf[...], wo_ref[...])
    o_ref[...] = x_ref[...] + _gated_norm(y, ng_ref[1:2, :], mod_ref[2:3, :])


def _attn_first_kernel(x_ref, q_ref, k_ref, v_ref, bias_ref, wo_ref, mod_ref, ng_ref, o_ref,
                       obuf, q2buf, sbuf, pbuf, kwin, vwin):
    tm, d = q_ref.shape
    for src, win in ((k_ref, kwin), (v_ref, vwin)):
        win[0:tm, :] = jnp.zeros((tm, d), BF16)
        win[tm:2 * tm, :] = src[...]
    _attn_tile(x_ref, q_ref, kwin, vwin, bias_ref, wo_ref, mod_ref, ng_ref, o_ref,
               obuf, q2buf, sbuf, pbuf, first_tile=True)


def _attn_rest_kernel(x_ref, q_ref, kwin, vwin, bias_ref, wo_ref, mod_ref, ng_ref, first_ref,
                      o_ref, obuf, q2buf, sbuf, pbuf):
    @pl.when(pl.program_id(1) == 0)
    def _():
        o_ref[...] = first_ref[...]

    @pl.when(pl.program_id(1) > 0)
    def _():
        _attn_tile(x_ref, q_ref, kwin.at[0], vwin.at[0], bias_ref, wo_ref, mod_ref, ng_ref,
                   o_ref, obuf, q2buf, sbuf, pbuf, first_tile=False)


def _attn(x, q, k, v, bias, w_o, mod, norm_g, layer):
    bsz, s, d = x.shape
    tm = ROW_TILE
    n_tiles = s // tm
    assert tm == N_LEFT_CHUNKS * CHUNK and tm % (GROUP * CHUNK) == 0
    n_pairs = d // V7X_LANES
    m_rows = 2 * GROUP * CHUNK
    scratch = [pltpu.VMEM((tm, d), BF16),
               pltpu.VMEM((2, m_rows, V7X_LANES), BF16),
               pltpu.VMEM((2, m_rows, WIN), F32),
               pltpu.VMEM((2, m_rows, WIN), BF16)]

    def call(first_tile, first_out=None):
        row_spec = pl.BlockSpec((None, tm, d), lambda b, i: (b, i, 0))
        if first_tile:
            kv_spec = row_spec
        else:
            kv_spec = pl.BlockSpec((pl.Element(1), pl.Element(2 * tm), pl.Element(d)),
                                   lambda b, i: (b, jnp.maximum(i - 1, 0) * tm, 0))
        in_specs = [
            row_spec, row_spec, kv_spec, kv_spec,
            _resident((2, n_pairs, 2 * CHUNK, SLAB), lambda b, i: (0, 0, 0, 0)),
            _resident((d, d), lambda b, i: (0, 0)),
            pl.BlockSpec((None, None, 6, d), lambda b, i: (layer, b, 0, 0)),
            _resident((None, 4, d), lambda b, i: (layer, 0, 0)),
        ]
        args = [x, q, k, v, bias, w_o, mod, norm_g]
        if first_tile:
            kv_scratch = [pltpu.VMEM((2 * tm, d), BF16), pltpu.VMEM((2 * tm, d), BF16)]
        else:
            in_specs.append(pl.BlockSpec((None, tm, d), lambda b, i: (b, 0, 0)))
            args.append(first_out)
            kv_scratch = []
        return pl.pallas_call(
            _attn_first_kernel if first_tile else _attn_rest_kernel,
            grid=(bsz, 1 if first_tile else n_tiles),
            in_specs=in_specs,
            out_specs=row_spec,
            out_shape=jax.ShapeDtypeStruct((bsz, tm if first_tile else s, d), F32),
            scratch_shapes=scratch + kv_scratch,
            compiler_params=_params(2),
            name="band_attention_first" if first_tile else "band_attention",
        )(*args)

    return call(False, call(True))


def kernel(x, c, mod_w, mod_b, norm_g, ffn_w_in, ffn_w_out, conv_w_in, conv_k, conv_w_out,
           kv_mod_w, kv_mod_b, kv_norm_g, w_kv, attn_w_q, attn_w_o, rel_bias):
    bsz, s, d = x.shape
    depth = mod_w.shape[0]
    n_a = conv_w_in.shape[0]
    n_heads = rel_bias.shape[1]
    assert d // n_heads == HEAD_DIM and s % ROW_TILE == 0

    mod = _modvec(c, mod_w, mod_b.reshape(depth, 1, 6 * d)).reshape(depth, bsz, 6, d)
    kvmod = _modvec(c, kv_mod_w[None], kv_mod_b.reshape(1, 1, 2 * d)).reshape(1, bsz, 2, d)

    conv_w_in, conv_w_out = conv_w_in.astype(BF16), conv_w_out.astype(BF16)
    kv_norm_g = kv_norm_g.reshape(1, d)
    w_kv = w_kv[None]

    k = v = None
    attn_w = None
    for layer in range(depth):
        ffn_casts = [(ffn_w_in, layer), (ffn_w_out, layer)]
        if layer < n_a:
            x, (w_in, w_out) = _mixer(x, mod, norm_g, layer, conv_w_in, conv_k, conv_w_out,
                                      layer, ffn_casts)
        else:
            bi = layer - n_a
            if attn_w is None:
                attn_w = (w_kv[0].astype(BF16), attn_w_q[bi].astype(BF16), attn_w_o[bi].astype(BF16))
            w_kv_b, w_q_b, w_o_b = attn_w
            (k_new, v_new, q), (w_in, w_out) = _proj(x, mod, kvmod, norm_g, layer, kv_norm_g,
                                                     w_kv_b, w_q_b, ffn_casts)
            if layer == n_a:
                k, v = k_new, v_new
            bias = _bias_table(rel_bias, bi).reshape(2, n_heads // 2, 2 * CHUNK, SLAB)
            x = _attn(x, q, k, v, bias, w_o_b, mod, norm_g, layer)
        nxt = layer + 1
        attn_casts = []
        if max(n_a, 1) <= nxt < depth:
            attn_casts = [(w_kv, 0), (attn_w_q, nxt - n_a), (attn_w_o, nxt - n_a)]
        x, attn_w = _ffn(x, mod, norm_g, layer, w_in, w_out, attn_casts)
        attn_w = tuple(attn_w) if attn_casts else None
    return x
```

```python
import functools
import math

import jax
import jax.numpy as jnp
from jax import lax
from jax.experimental import pallas as pl
from jax.experimental.pallas import tpu as pltpu

EPS = 1e-6
CHUNK = 64
N_LEFT_CHUNKS = 8
BAND = (N_LEFT_CHUNKS + 1) * CHUNK
SLAB = BAND + CHUNK
GROUP = 4
WIN = (GROUP + N_LEFT_CHUNKS) * CHUNK
HEAD_DIM = 64
MAX_REL = 2 * CHUNK
LOG2_E = math.log2(math.e)

V7X_LANES = 128
V7X_BF16_SUBLANES = 16
V7X_MXU_COLS = 256
ROW_TILE = 512
V7X_VMEM_LIMIT = 56 * 1024 * 1024

BF16 = jnp.bfloat16
F32 = jnp.float32


def _dot(a, b):
    return jnp.dot(a, b, preferred_element_type=F32)


def _unit_rms(x):
    return x * lax.rsqrt(jnp.mean(x * x, axis=-1, keepdims=True) + EPS)


def _norm_modulate(xn, g, scale, shift):
    return xn * (g * (1.0 + scale)) + shift


def _gated_norm(y, g, gate):
    return _unit_rms(y) * (gate * g)


def _resident(block_shape, index_map):
    return pl.BlockSpec(block_shape, index_map, pipeline_mode=pl.Buffered(1))


def _params(n_grid_dims):
    return pltpu.CompilerParams(
        dimension_semantics=("arbitrary",) * n_grid_dims,
        vmem_limit_bytes=V7X_VMEM_LIMIT)


def _with_casts(body, n_in, n_out, n_cast):
    def wrapped(*refs):
        srcs = refs[n_in:n_in + n_cast]
        outs_end = n_in + n_cast + n_out
        dsts = refs[outs_end:outs_end + n_cast]
        for src, dst in zip(srcs, dsts):
            dst[...] = src[...].astype(BF16)
        body(*refs[:n_in], *refs[n_in + n_cast:outs_end], *refs[outs_end + n_cast:])
    return wrapped


def _tiled_call(body, name, grid, in_specs, args, out_specs, out_shapes, scratch=(), casts=()):
    bsz, n_tiles = grid
    c_in, c_out, c_shapes, c_args = [], [], [], []
    for w, idx in casts:
        r, c = w.shape[1:]
        rows, cols = r // n_tiles, c // bsz
        assert rows * n_tiles == r and rows % V7X_BF16_SUBLANES == 0
        assert cols * bsz == c and cols % V7X_LANES == 0
        c_in.append(pl.BlockSpec((None, rows, cols), lambda b, i, idx=idx: (idx, i, b)))
        c_out.append(pl.BlockSpec((rows, cols), lambda b, i: (i, b)))
        c_shapes.append(jax.ShapeDtypeStruct((r, c), BF16))
        c_args.append(w)
    res = pl.pallas_call(
        _with_casts(body, len(in_specs), len(out_specs), len(casts)),
        grid=grid,
        in_specs=list(in_specs) + c_in,
        out_specs=list(out_specs) + c_out,
        out_shape=list(out_shapes) + c_shapes,
        scratch_shapes=list(scratch),
        compiler_params=_params(len(grid)),
        name=name,
    )(*args, *c_args)
    return res[:len(out_specs)], res[len(out_specs):]


def _modvec_kernel(c_ref, w_ref, b_ref, o_ref):
    s = jax.nn.silu(c_ref[...]).astype(BF16)
    o_ref[...] = _dot(s, w_ref[...].astype(BF16)) + b_ref[...]


def _modvec(c, w, b, col_tile=1024):
    n_layers, d, n = w.shape
    bsz = c.shape[0]
    return pl.pallas_call(
        _modvec_kernel,
        grid=(n_layers, n // col_tile),
        in_specs=[
            pl.BlockSpec((bsz, d), lambda l, j: (0, 0)),
            pl.BlockSpec((None, d, col_tile), lambda l, j: (l, 0, j)),
            pl.BlockSpec((None, 1, col_tile), lambda l, j: (l, 0, j)),
        ],
        out_specs=pl.BlockSpec((None, bsz, col_tile), lambda l, j: (l, 0, j)),
        out_shape=jax.ShapeDtypeStruct((n_layers, bsz, n), F32),
        compiler_params=_params(2),
        name="modvec",
    )(c, w, b)


def _mixer_kernel(x_ref, mod_ref, ng_ref, win_ref, ck_ref, wout_ref, o_ref, zbuf, *, n_sub):
    tm, d = x_ref.shape
    sh1, sc1, g1 = mod_ref[0:1, :], mod_ref[1:2, :], mod_ref[2:3, :]

    @pl.when(pl.program_id(1) == 0)
    def _():
        zbuf[0:8, :] = jnp.zeros((8, d), F32)

    rows = tm // n_sub
    for r0 in range(0, tm, rows):
        x = x_ref[r0:r0 + rows, :]
        h = _norm_modulate(_unit_rms(x), ng_ref[0:1, :], sc1, sh1).astype(BF16)
        us = []
        for c0 in range(0, d, V7X_MXU_COLS):
            cs = slice(c0, c0 + V7X_MXU_COLS)
            z = (_dot(h, win_ref[:, d + c0:d + c0 + V7X_MXU_COLS])
                 * _dot(h, win_ref[:, 2 * d + c0:2 * d + c0 + V7X_MXU_COLS]))
            zbuf[8 + r0:8 + r0 + rows, cs] = z
            conv = (ck_ref[0:1, cs] * zbuf[6 + r0:6 + r0 + rows, cs]
                    + ck_ref[1:2, cs] * zbuf[7 + r0:7 + r0 + rows, cs]
                    + ck_ref[2:3, cs] * z)
            us.append((_dot(h, win_ref[:, cs]) * conv).astype(BF16))
        y = _dot(jnp.concatenate(us, axis=1), wout_ref[...])
        o_ref[r0:r0 + rows, :] = x + _gated_norm(y, ng_ref[1:2, :], g1)
    zbuf[0:8, :] = zbuf[tm:tm + 8, :]


def _mixer(x, mod, norm_g, layer, w_in, conv_k, w_out, a_idx, casts):
    bsz, s, d = x.shape
    tm = ROW_TILE
    row_spec = pl.BlockSpec((None, tm, d), lambda b, i: (b, i, 0))
    (out,), cast = _tiled_call(
        functools.partial(_mixer_kernel, n_sub=2), "conv_mixer", (bsz, s // tm),
        in_specs=[
            row_spec,
            pl.BlockSpec((None, None, 6, d), lambda b, i: (layer, b, 0, 0)),
            _resident((None, 4, d), lambda b, i: (layer, 0, 0)),
            _resident((None, d, 3 * d), lambda b, i: (a_idx, 0, 0)),
            _resident((None, 3, d), lambda b, i: (a_idx, 0, 0)),
            _resident((None, d, d), lambda b, i: (a_idx, 0, 0)),
        ],
        args=(x, mod, norm_g, w_in, conv_k, w_out),
        out_specs=[row_spec],
        out_shapes=[jax.ShapeDtypeStruct(x.shape, F32)],
        scratch=[pltpu.VMEM((tm + 8, d), F32)],
        casts=casts)
    return out, cast


def _ffn_kernel(x_ref, mod_ref, ng_ref, win_ref, wout_ref, o_ref, *, n_sub):
    sh2, sc2, g2 = mod_ref[3:4, :], mod_ref[4:5, :], mod_ref[5:6, :]
    f = wout_ref.shape[0]
    rows = x_ref.shape[0] // n_sub
    for r in range(n_sub):
        rs = slice(r * rows, (r + 1) * rows)
        x = x_ref[rs, :]
        h = _norm_modulate(_unit_rms(x), ng_ref[2:3, :], sc2, sh2).astype(BF16)
        g = _dot(h, win_ref[:, 0:f])
        u = _dot(h, win_ref[:, f:2 * f])
        y = _dot((jax.nn.silu(g) * u).astype(BF16), wout_ref[...])
        o_ref[rs, :] = x + _gated_norm(y, ng_ref[3:4, :], g2)


def _ffn(x, mod, norm_g, layer, w_in, w_out, casts):
    bsz, s, d = x.shape
    f = w_out.shape[0]
    tm = ROW_TILE
    row_spec = pl.BlockSpec((None, tm, d), lambda b, i: (b, i, 0))
    (out,), cast = _tiled_call(
        functools.partial(_ffn_kernel, n_sub=2), "swiglu_ffn", (bsz, s // tm),
        in_specs=[
            row_spec,
            pl.BlockSpec((None, None, 6, d), lambda b, i: (layer, b, 0, 0)),
            _resident((None, 4, d), lambda b, i: (layer, 0, 0)),
            _resident((d, 2 * f), lambda b, i: (0, 0)),
            _resident((f, d), lambda b, i: (0, 0)),
        ],
        args=(x, mod, norm_g, w_in, w_out),
        out_specs=[row_spec],
        out_shapes=[jax.ShapeDtypeStruct(x.shape, F32)],
        casts=casts)
    return out, cast


def _proj_kernel(x_ref, mod_ref, kvmod_ref, ng_ref, kvg_ref, wkv_ref, wq_ref,
                 k_ref, v_ref, q_ref):
    d = x_ref.shape[1]
    xn = _unit_rms(x_ref[...])
    hkv = _norm_modulate(xn, kvg_ref[...], kvmod_ref[1:2, :], kvmod_ref[0:1, :]).astype(BF16)
    k_ref[...] = _dot(hkv, wkv_ref[:, 0:d]).astype(BF16)
    v_ref[...] = _dot(hkv, wkv_ref[:, d:2 * d]).astype(BF16)
    h = _norm_modulate(xn, ng_ref[0:1, :], mod_ref[1:2, :], mod_ref[0:1, :]).astype(BF16)
    q_ref[...] = (_dot(h, wq_ref[...]) * (LOG2_E * HEAD_DIM ** -0.5)).astype(BF16)


def _proj(x, mod, kvmod, norm_g, layer, kv_norm_g, w_kv, w_q, casts):
    bsz, s, d = x.shape
    tm = ROW_TILE
    row_spec = pl.BlockSpec((None, tm, d), lambda b, i: (b, i, 0))
    out = jax.ShapeDtypeStruct(x.shape, BF16)
    return _tiled_call(
        _proj_kernel, "qkv_proj", (bsz, s // tm),
        in_specs=[
            row_spec,
            pl.BlockSpec((None, None, 6, d), lambda b, i: (layer, b, 0, 0)),
            pl.BlockSpec((None, None, 2, d), lambda b, i: (0, b, 0, 0)),
            _resident((None, 4, d), lambda b, i: (layer, 0, 0)),
            _resident((1, d), lambda b, i: (0, 0)),
            _resident((d, 2 * d), lambda b, i: (0, 0)),
            _resident((d, d), lambda b, i: (0, 0)),
        ],
        args=(x, mod, kvmod, norm_g, kv_norm_g, w_kv, w_q),
        out_specs=[row_spec, row_spec, row_spec],
        out_shapes=[out, out, out],
        casts=casts)


def _bias_kernel(rb_ref, o_ref):
    n_heads = rb_ref.shape[0]
    n_tab = 2 * MAX_REL
    rb = rb_ref[:, 0:n_tab]
    hi = rb.astype(BF16)
    r1 = rb - hi.astype(F32)
    mid = r1.astype(BF16)
    lo = (r1 - mid.astype(F32)).astype(BF16)
    pieces = jnp.concatenate([hi, mid, lo], axis=0)
    rb_top = rb_ref[:, n_tab:n_tab + 1]
    u = lax.broadcasted_iota(jnp.int32, (1, WIN), 1)
    m = jnp.where(u >= WIN - CHUNK, u - WIN, u)
    idx = jnp.clip(N_LEFT_CHUNKS * CHUNK - m, -MAX_REL, MAX_REL) + MAX_REL
    row = lax.broadcasted_iota(jnp.int32, (n_tab, WIN), 0)
    onehot = jnp.where(row == idx, 1.0, 0.0).astype(BF16)
    t3 = _dot(pieces, onehot)
    w = t3[0:n_heads] + t3[n_heads:2 * n_heads] + t3[2 * n_heads:3 * n_heads]
    w = (w + jnp.where(idx == n_tab, rb_top, 0.0)) * LOG2_E
    col = lax.broadcasted_iota(jnp.int32, (CHUNK, WIN), 1)
    neg = jnp.finfo(F32).min
    for var in range(2):
        kk = col - var * CHUNK
        in_band = (kk >= 0) & (kk < BAND)
        for h in range(n_heads):
            t = pltpu.roll(jnp.broadcast_to(w[h:h + 1, :], (CHUNK, WIN)), var * CHUNK, 1,
                           stride=1, stride_axis=0)
            o_ref[var, h] = jnp.where(in_band, t, neg)[:, 0:SLAB]


def _bias_table(rel_bias, b_idx):
    n_heads, n_rel = rel_bias.shape[1:]
    return pl.pallas_call(
        _bias_kernel,
        grid=(1,),
        in_specs=[pl.BlockSpec((None, n_heads, n_rel), lambda i: (b_idx, 0, 0))],
        out_specs=pl.BlockSpec((2, n_heads, CHUNK, SLAB), lambda i: (0, 0, 0, 0)),
        out_shape=jax.ShapeDtypeStruct((2, n_heads, CHUNK, SLAB), F32),
        compiler_params=_params(1),
        name="rel_bias_table",
    )(rel_bias)


def _attn_tile(x_ref, q_ref, kwin, vwin, bias_ref, wo_ref, mod_ref, ng_ref, o_ref,
               obuf, q2buf, sbuf, pbuf, *, first_tile):
    tm, d = q_ref.shape
    low_head = lax.broadcasted_iota(jnp.int32, (CHUNK, V7X_LANES), 1) < HEAD_DIM
    neg = jnp.finfo(F32).min
    ones = jnp.ones((WIN, V7X_LANES), BF16)
    half = GROUP * CHUNK
    for slot in range(2):
        pbuf[slot, 0:half, SLAB:WIN] = jnp.zeros((half, WIN - SLAB), BF16)
        pbuf[slot, half:2 * half, 0:V7X_LANES] = jnp.zeros((half, V7X_LANES), BF16)
    body = 0
    for g in range(tm // (GROUP * CHUNK)):
        w0 = g * GROUP * CHUNK
        for p in range(d // V7X_LANES):
            ls = slice(p * V7X_LANES, (p + 1) * V7X_LANES)
            slot = body % 2
            body += 1
            for j in range(GROUP):
                qp = q_ref[w0 + j * CHUNK:w0 + (j + 1) * CHUNK, ls]
                zero = jnp.zeros_like(qp)
                q2buf[slot, (2 * j) * CHUNK:(2 * j + 1) * CHUNK, :] = jnp.where(low_head, qp, zero)
                q2buf[slot, (2 * j + 1) * CHUNK:(2 * j + 2) * CHUNK, :] = jnp.where(low_head, zero, qp)
            sbuf[slot] = lax.dot_general(q2buf[slot], kwin[w0:w0 + WIN, ls],
                                         (((1,), (1,)), ((), ())), preferred_element_type=F32)
            if first_tile:
                sbuf[slot, :, 0:tm - w0] = jnp.full((2 * half, tm - w0), neg, F32)
            for j in range(GROUP):
                t0 = (j // 2) * V7X_LANES
                for e in range(2):
                    rows = slice((2 * j + e) * CHUNK, (2 * j + e + 1) * CHUNK)
                    s = sbuf[slot, rows, t0:t0 + SLAB] + bias_ref[j % 2, p, e * CHUNK:(e + 1) * CHUNK, :]
                    ex = jnp.exp2(s - jnp.max(s, axis=-1, keepdims=True))
                    pbuf[slot, rows, t0:t0 + SLAB] = ex.astype(BF16)
            v_ones = jnp.concatenate([vwin[w0:w0 + WIN, ls], ones], axis=1)
            o2 = _dot(pbuf[slot], v_ones)
            o2 = o2[:, 0:V7X_LANES] / o2[:, V7X_LANES:2 * V7X_LANES]
            for j in range(GROUP):
                lo = o2[(2 * j) * CHUNK:(2 * j + 1) * CHUNK, :]
                hi = o2[(2 * j + 1) * CHUNK:(2 * j + 2) * CHUNK, :]
                obuf[w0 + j * CHUNK:w0 + (j + 1) * CHUNK, ls] = jnp.where(low_head, lo, hi).astype(BF16)
    y = _dot(obuf[...], wo_ref[...])
    o_ref[...] = x_ref[...] + _gated_norm(y, ng_ref[1:2, :], mod_ref[2:3, :])


def _attn_first_kernel(x_ref, q_ref, k_ref, v_ref, bias_ref, wo_ref, mod_ref, ng_ref, o_ref,
                       obuf, q2buf, sbuf, pbuf, kwin, vwin):
    tm, d = q_ref.shape
    for src, win in ((k_ref, kwin), (v_ref, vwin)):
        win[0:tm, :] = jnp.zeros((tm, d), BF16)
        win[tm:2 * tm, :] = src[...]
    _attn_tile(x_ref, q_ref, kwin, vwin, bias_ref, wo_ref, mod_ref, ng_ref, o_ref,
               obuf, q2buf, sbuf, pbuf, first_tile=True)


def _attn_rest_kernel(x_ref, q_ref, kwin, vwin, bias_ref, wo_ref, mod_ref, ng_ref, first_ref,
                      o_ref, obuf, q2buf, sbuf, pbuf):
    @pl.when(pl.program_id(1) == 0)
    def _():
        o_ref[...] = first_ref[...]

    @pl.when(pl.program_id(1) > 0)
    def _():
        _attn_tile(x_ref, q_ref, kwin.at[0], vwin.at[0], bias_ref, wo_ref, mod_ref, ng_ref,
                   o_ref, obuf, q2buf, sbuf, pbuf, first_tile=False)


def _attn(x, q, k, v, bias, w_o, mod, norm_g, layer):
    bsz, s, d = x.shape
    tm = ROW_TILE
    n_tiles = s // tm
    assert tm == N_LEFT_CHUNKS * CHUNK and tm % (GROUP * CHUNK) == 0
    n_pairs = d // V7X_LANES
    m_rows = 2 * GROUP * CHUNK
    scratch = [pltpu.VMEM((tm, d), BF16),
               pltpu.VMEM((2, m_rows, V7X_LANES), BF16),
               pltpu.VMEM((2, m_rows, WIN), F32),
               pltpu.VMEM((2, m_rows, WIN), BF16)]

    def call(first_tile, first_out=None):
        row_spec = pl.BlockSpec((None, tm, d), lambda b, i: (b, i, 0))
        if first_tile:
            kv_spec = row_spec
        else:
            kv_spec = pl.BlockSpec((pl.Element(1), pl.Element(2 * tm), pl.Element(d)),
                                   lambda b, i: (b, jnp.maximum(i - 1, 0) * tm, 0))
        in_specs = [
            row_spec, row_spec, kv_spec, kv_spec,
            _resident((2, n_pairs, 2 * CHUNK, SLAB), lambda b, i: (0, 0, 0, 0)),
            _resident((d, d), lambda b, i: (0, 0)),
            pl.BlockSpec((None, None, 6, d), lambda b, i: (layer, b, 0, 0)),
            _resident((None, 4, d), lambda b, i: (layer, 0, 0)),
        ]
        args = [x, q, k, v, bias, w_o, mod, norm_g]
        if first_tile:
            kv_scratch = [pltpu.VMEM((2 * tm, d), BF16), pltpu.VMEM((2 * tm, d), BF16)]
        else:
            in_specs.append(pl.BlockSpec((None, tm, d), lambda b, i: (b, 0, 0)))
            args.append(first_out)
            kv_scratch = []
        return pl.pallas_call(
            _attn_first_kernel if first_tile else _attn_rest_kernel,
            grid=(bsz, 1 if first_tile else n_tiles),
            in_specs=in_specs,
            out_specs=row_spec,
            out_shape=jax.ShapeDtypeStruct((bsz, tm if first_tile else s, d), F32),
            scratch_shapes=scratch + kv_scratch,
            compiler_params=_params(2),
            name="band_attention_first" if first_tile else "band_attention",
        )(*args)

    return call(False, call(True))


def kernel(x, c, mod_w, mod_b, norm_g, ffn_w_in, ffn_w_out, conv_w_in, conv_k, conv_w_out,
           kv_mod_w, kv_mod_b, kv_norm_g, w_kv, attn_w_q, attn_w_o, rel_bias):
    bsz, s, d = x.shape
    depth = mod_w.shape[0]
    n_a = conv_w_in.shape[0]
    n_heads = rel_bias.shape[1]
    assert d // n_heads == HEAD_DIM and s % ROW_TILE == 0

    mod = _modvec(c, mod_w, mod_b.reshape(depth, 1, 6 * d)).reshape(depth, bsz, 6, d)
    kvmod = _modvec(c, kv_mod_w[None], kv_mod_b.reshape(1, 1, 2 * d)).reshape(1, bsz, 2, d)

    conv_w_in, conv_w_out = conv_w_in.astype(BF16), conv_w_out.astype(BF16)
    kv_norm_g = kv_norm_g.reshape(1, d)
    w_kv = w_kv[None]

    k = v = None
    attn_w = None
    for layer in range(depth):
        ffn_casts = [(ffn_w_in, layer), (ffn_w_out, layer)]
        if layer < n_a:
            x, (w_in, w_out) = _mixer(x, mod, norm_g, layer, conv_w_in, conv_k, conv_w_out,
                                      layer, ffn_casts)
        else:
            bi = layer - n_a
            if attn_w is None:
                attn_w = (w_kv[0].astype(BF16), attn_w_q[bi].astype(BF16), attn_w_o[bi].astype(BF16))
            w_kv_b, w_q_b, w_o_b = attn_w
            (k_new, v_new, q), (w_in, w_out) = _proj(x, mod, kvmod, norm_g, layer, kv_norm_g,
                                                     w_kv_b, w_q_b, ffn_casts)
            if layer == n_a:
                k, v = k_new, v_new
            bias = _bias_table(rel_bias, bi).reshape(2, n_heads // 2, 2 * CHUNK, SLAB)
            x = _attn(x, q, k, v, bias, w_o_b, mod, norm_g, layer)
        nxt = layer + 1
        attn_casts = []
        if max(n_a, 1) <= nxt < depth:
            attn_casts = [(w_kv, 0), (attn_w_q, nxt - n_a), (attn_w_o, nxt - n_a)]
        x, attn_w = _ffn(x, mod, norm_g, layer, w_in, w_out, attn_casts)
        attn_w = tuple(attn_w) if attn_casts else None
    return x
```

```python
import functools
import math

import jax
import jax.numpy as jnp
from jax import lax
from jax.experimental import pallas as pl
from jax.experimental.pallas import tpu as pltpu

EPS = 1e-6
CHUNK = 64
N_LEFT_CHUNKS = 8
BAND = (N_LEFT_CHUNKS + 1) * CHUNK
SLAB = BAND + CHUNK
GROUP = 4
SCORES_AHEAD = 1
SOFTMAX_ROWS = 32
WIN = (GROUP + N_LEFT_CHUNKS) * CHUNK
HEAD_DIM = 64
MAX_REL = 2 * CHUNK
LOG2_E = math.log2(math.e)

V7X_LANES = 128
V7X_BF16_SUBLANES = 16
V7X_MXU_COLS = 256
ROW_TILE = 512
V7X_VMEM_LIMIT = 56 * 1024 * 1024

BF16 = jnp.bfloat16
F32 = jnp.float32


def _dot(a, b):
    return jnp.dot(a, b, preferred_element_type=F32)


def _unit_rms(x):
    return x * lax.rsqrt(jnp.mean(x * x, axis=-1, keepdims=True) + EPS)


def _norm_modulate(xn, g, scale, shift):
    return xn * (g * (1.0 + scale)) + shift


def _gated_norm(y, g, gate):
    return _unit_rms(y) * (gate * g)


def _resident(block_shape, index_map):
    return pl.BlockSpec(block_shape, index_map, pipeline_mode=pl.Buffered(1))


def _params(n_grid_dims):
    return pltpu.CompilerParams(
        dimension_semantics=("arbitrary",) * n_grid_dims,
        vmem_limit_bytes=V7X_VMEM_LIMIT)


def _with_casts(body, n_in, n_out, n_cast):
    def wrapped(*refs):
        srcs = refs[n_in:n_in + n_cast]
        outs_end = n_in + n_cast + n_out
        dsts = refs[outs_end:outs_end + n_cast]
        for src, dst in zip(srcs, dsts):
            dst[...] = src[...].astype(BF16)
        body(*refs[:n_in], *refs[n_in + n_cast:outs_end], *refs[outs_end + n_cast:])
    return wrapped


def _tiled_call(body, name, grid, in_specs, args, out_specs, out_shapes, scratch=(), casts=()):
    bsz, n_tiles = grid
    c_in, c_out, c_shapes, c_args = [], [], [], []
    for w, idx in casts:
        r, c = w.shape[1:]
        rows, cols = r // n_tiles, c // bsz
        assert rows * n_tiles == r and rows % V7X_BF16_SUBLANES == 0
        assert cols * bsz == c and cols % V7X_LANES == 0
        c_in.append(pl.BlockSpec((None, rows, cols), lambda b, i, idx=idx: (idx, i, b)))
        c_out.append(pl.BlockSpec((rows, cols), lambda b, i: (i, b)))
        c_shapes.append(jax.ShapeDtypeStruct((r, c), BF16))
        c_args.append(w)
    res = pl.pallas_call(
        _with_casts(body, len(in_specs), len(out_specs), len(casts)),
        grid=grid,
        in_specs=list(in_specs) + c_in,
        out_specs=list(out_specs) + c_out,
        out_shape=list(out_shapes) + c_shapes,
        scratch_shapes=list(scratch),
        compiler_params=_params(len(grid)),
        name=name,
    )(*args, *c_args)
    return res[:len(out_specs)], res[len(out_specs):]


def _modvec_kernel(c_ref, w_ref, b_ref, o_ref):
    s = jax.nn.silu(c_ref[...]).astype(BF16)
    o_ref[...] = _dot(s, w_ref[...].astype(BF16)) + b_ref[...]


def _modvec(c, w, b, col_tile=1024):
    n_layers, d, n = w.shape
    bsz = c.shape[0]
    return pl.pallas_call(
        _modvec_kernel,
        grid=(n_layers, n // col_tile),
        in_specs=[
            pl.BlockSpec((bsz, d), lambda l, j: (0, 0)),
            pl.BlockSpec((None, d, col_tile), lambda l, j: (l, 0, j)),
            pl.BlockSpec((None, 1, col_tile), lambda l, j: (l, 0, j)),
        ],
        out_specs=pl.BlockSpec((None, bsz, col_tile), lambda l, j: (l, 0, j)),
        out_shape=jax.ShapeDtypeStruct((n_layers, bsz, n), F32),
        compiler_params=_params(2),
        name="modvec",
    )(c, w, b)


def _mixer_kernel(x_ref, mod_ref, ng_ref, win_ref, ck_ref, wout_ref, o_ref, zbuf, *, n_sub):
    tm, d = x_ref.shape
    sh1, sc1, g1 = mod_ref[0:1, :], mod_ref[1:2, :], mod_ref[2:3, :]

    @pl.when(pl.program_id(1) == 0)
    def _():
        zbuf[0:8, :] = jnp.zeros((8, d), F32)

    rows = tm // n_sub
    for r0 in range(0, tm, rows):
        x = x_ref[r0:r0 + rows, :]
        h = _norm_modulate(_unit_rms(x), ng_ref[0:1, :], sc1, sh1).astype(BF16)
        us = []
        for c0 in range(0, d, V7X_MXU_COLS):
            cs = slice(c0, c0 + V7X_MXU_COLS)
            z = (_dot(h, win_ref[:, d + c0:d + c0 + V7X_MXU_COLS])
                 * _dot(h, win_ref[:, 2 * d + c0:2 * d + c0 + V7X_MXU_COLS]))
            zbuf[8 + r0:8 + r0 + rows, cs] = z
            conv = (ck_ref[0:1, cs] * zbuf[6 + r0:6 + r0 + rows, cs]
                    + ck_ref[1:2, cs] * zbuf[7 + r0:7 + r0 + rows, cs]
                    + ck_ref[2:3, cs] * z)
            us.append((_dot(h, win_ref[:, cs]) * conv).astype(BF16))
        y = _dot(jnp.concatenate(us, axis=1), wout_ref[...])
        o_ref[r0:r0 + rows, :] = x + _gated_norm(y, ng_ref[1:2, :], g1)
    zbuf[0:8, :] = zbuf[tm:tm + 8, :]


def _mixer(x, mod, norm_g, layer, w_in, conv_k, w_out, a_idx, casts):
    bsz, s, d = x.shape
    tm = ROW_TILE
    row_spec = pl.BlockSpec((None, tm, d), lambda b, i: (b, i, 0))
    (out,), cast = _tiled_call(
        functools.partial(_mixer_kernel, n_sub=2), "conv_mixer", (bsz, s // tm),
        in_specs=[
            row_spec,
            pl.BlockSpec((None, None, 6, d), lambda b, i: (layer, b, 0, 0)),
            _resident((None, 4, d), lambda b, i: (layer, 0, 0)),
            _resident((None, d, 3 * d), lambda b, i: (a_idx, 0, 0)),
            _resident((None, 3, d), lambda b, i: (a_idx, 0, 0)),
            _resident((None, d, d), lambda b, i: (a_idx, 0, 0)),
        ],
        args=(x, mod, norm_g, w_in, conv_k, w_out),
        out_specs=[row_spec],
        out_shapes=[jax.ShapeDtypeStruct(x.shape, F32)],
        scratch=[pltpu.VMEM((tm + 8, d), F32)],
        casts=casts)
    return out, cast


def _ffn_kernel(x_ref, mod_ref, ng_ref, win_ref, wout_ref, o_ref, *, n_sub):
    sh2, sc2, g2 = mod_ref[3:4, :], mod_ref[4:5, :], mod_ref[5:6, :]
    f = wout_ref.shape[0]
    rows = x_ref.shape[0] // n_sub
    for r in range(n_sub):
        rs = slice(r * rows, (r + 1) * rows)
        x = x_ref[rs, :]
        h = _norm_modulate(_unit_rms(x), ng_ref[2:3, :], sc2, sh2).astype(BF16)
        g = _dot(h, win_ref[:, 0:f])
        u = _dot(h, win_ref[:, f:2 * f])
        y = _dot((jax.nn.silu(g) * u).astype(BF16), wout_ref[...])
        o_ref[rs, :] = x + _gated_norm(y, ng_ref[3:4, :], g2)


def _ffn(x, mod, norm_g, layer, w_in, w_out, casts):
    bsz, s, d = x.shape
    f = w_out.shape[0]
    tm = ROW_TILE
    row_spec = pl.BlockSpec((None, tm, d), lambda b, i: (b, i, 0))
    (out,), cast = _tiled_call(
        functools.partial(_ffn_kernel, n_sub=2), "swiglu_ffn", (bsz, s // tm),
        in_specs=[
            row_spec,
            pl.BlockSpec((None, None, 6, d), lambda b, i: (layer, b, 0, 0)),
            _resident((None, 4, d), lambda b, i: (layer, 0, 0)),
            _resident((d, 2 * f), lambda b, i: (0, 0)),
            _resident((f, d), lambda b, i: (0, 0)),
        ],
        args=(x, mod, norm_g, w_in, w_out),
        out_specs=[row_spec],
        out_shapes=[jax.ShapeDtypeStruct(x.shape, F32)],
        casts=casts)
    return out, cast


def _proj_kernel(x_ref, mod_ref, kvmod_ref, ng_ref, kvg_ref, wkv_ref, wq_ref,
                 k_ref, v_ref, q_ref):
    d = x_ref.shape[1]
    xn = _unit_rms(x_ref[...])
    hkv = _norm_modulate(xn, kvg_ref[...], kvmod_ref[1:2, :], kvmod_ref[0:1, :]).astype(BF16)
    k_ref[...] = _dot(hkv, wkv_ref[:, 0:d]).astype(BF16)
    v_ref[...] = _dot(hkv, wkv_ref[:, d:2 * d]).astype(BF16)
    h = _norm_modulate(xn, ng_ref[0:1, :], mod_ref[1:2, :], mod_ref[0:1, :]).astype(BF16)
    q_ref[...] = (_dot(h, wq_ref[...]) * (LOG2_E * HEAD_DIM ** -0.5)).astype(BF16)


def _proj(x, mod, kvmod, norm_g, layer, kv_norm_g, w_kv, w_q, casts):
    bsz, s, d = x.shape
    tm = ROW_TILE
    row_spec = pl.BlockSpec((None, tm, d), lambda b, i: (b, i, 0))
    out = jax.ShapeDtypeStruct(x.shape, BF16)
    return _tiled_call(
        _proj_kernel, "qkv_proj", (bsz, s // tm),
        in_specs=[
            row_spec,
            pl.BlockSpec((None, None, 6, d), lambda b, i: (layer, b, 0, 0)),
            pl.BlockSpec((None, None, 2, d), lambda b, i: (0, b, 0, 0)),
            _resident((None, 4, d), lambda b, i: (layer, 0, 0)),
            _resident((1, d), lambda b, i: (0, 0)),
            _resident((d, 2 * d), lambda b, i: (0, 0)),
            _resident((d, d), lambda b, i: (0, 0)),
        ],
        args=(x, mod, kvmod, norm_g, kv_norm_g, w_kv, w_q),
        out_specs=[row_spec, row_spec, row_spec],
        out_shapes=[out, out, out],
        casts=casts)


def _bias_kernel(rb_ref, o_ref):
    n_heads = rb_ref.shape[0]
    n_tab = 2 * MAX_REL
    rb = rb_ref[:, 0:n_tab]
    hi = rb.astype(BF16)
    r1 = rb - hi.astype(F32)
    mid = r1.astype(BF16)
    lo = (r1 - mid.astype(F32)).astype(BF16)
    pieces = jnp.concatenate([hi, mid, lo], axis=0)
    rb_top = rb_ref[:, n_tab:n_tab + 1]
    u = lax.broadcasted_iota(jnp.int32, (1, WIN), 1)
    m = jnp.where(u >= WIN - CHUNK, u - WIN, u)
    idx = jnp.clip(N_LEFT_CHUNKS * CHUNK - m, -MAX_REL, MAX_REL) + MAX_REL
    row = lax.broadcasted_iota(jnp.int32, (n_tab, WIN), 0)
    onehot = jnp.where(row == idx, 1.0, 0.0).astype(BF16)
    t3 = _dot(pieces, onehot)
    w = t3[0:n_heads] + t3[n_heads:2 * n_heads] + t3[2 * n_heads:3 * n_heads]
    w = (w + jnp.where(idx == n_tab, rb_top, 0.0)) * LOG2_E
    col = lax.broadcasted_iota(jnp.int32, (CHUNK, WIN), 1)
    neg = jnp.finfo(F32).min
    for var in range(2):
        kk = col - var * CHUNK
        in_band = (kk >= 0) & (kk < BAND)
        for h in range(n_heads):
            t = pltpu.roll(jnp.broadcast_to(w[h:h + 1, :], (CHUNK, WIN)), var * CHUNK, 1,
                           stride=1, stride_axis=0)
            o_ref[var, h] = jnp.where(in_band, t, neg)[:, 0:SLAB]


def _bias_table(rel_bias, b_idx):
    n_heads, n_rel = rel_bias.shape[1:]
    return pl.pallas_call(
        _bias_kernel,
        grid=(1,),
        in_specs=[pl.BlockSpec((None, n_heads, n_rel), lambda i: (b_idx, 0, 0))],
        out_specs=pl.BlockSpec((2, n_heads, CHUNK, SLAB), lambda i: (0, 0, 0, 0)),
        out_shape=jax.ShapeDtypeStruct((2, n_heads, CHUNK, SLAB), F32),
        compiler_params=_params(1),
        name="rel_bias_table",
    )(rel_bias)


def _attn_tile(x_ref, q_ref, kwin, vwin, bias_ref, wo_ref, mod_ref, ng_ref, o_ref, obuf,
               *, first_tile):
    tm, d = q_ref.shape
    m_rows = 2 * GROUP * CHUNK
    low_head = lax.broadcasted_iota(jnp.int32, (CHUNK, V7X_LANES), 1) < HEAD_DIM
    neg = jnp.finfo(F32).min
    ones = jnp.ones((WIN, V7X_LANES), BF16)
    zcols = jnp.zeros((SOFTMAX_ROWS, WIN - SLAB), BF16)
    col = lax.broadcasted_iota(jnp.int32, (m_rows, WIN), 1) if first_tile else None

    def scores(g, p):
        w0 = g * GROUP * CHUNK
        ls = slice(p * V7X_LANES, (p + 1) * V7X_LANES)
        q2 = []
        for j in range(GROUP):
            qp = q_ref[w0 + j * CHUNK:w0 + (j + 1) * CHUNK, ls]
            zero = jnp.zeros_like(qp)
            q2 += [jnp.where(low_head, qp, zero), jnp.where(low_head, zero, qp)]
        s = lax.dot_general(jnp.concatenate(q2, axis=0), kwin[w0:w0 + WIN, ls],
                            (((1,), (1,)), ((), ())), preferred_element_type=F32)
        if first_tile:
            s = jnp.where(col < tm - w0, neg, s)
        return s

    def attend(g, p, s):
        w0 = g * GROUP * CHUNK
        ls = slice(p * V7X_LANES, (p + 1) * V7X_LANES)
        blocks = []
        for j in range(GROUP):
            t0 = (j // 2) * V7X_LANES
            for b0 in range(0, 2 * CHUNK, SOFTMAX_ROWS):
                r0 = 2 * j * CHUNK + b0
                sl = (s[r0:r0 + SOFTMAX_ROWS, t0:t0 + SLAB]
                      + bias_ref[j % 2, p, b0:b0 + SOFTMAX_ROWS, :])
                ex = jnp.exp2(sl - jnp.max(sl, axis=-1, keepdims=True)).astype(BF16)
                blocks.append(jnp.concatenate([ex, zcols] if t0 == 0 else [zcols, ex], axis=1))
        v_ones = jnp.concatenate([vwin[w0:w0 + WIN, ls], ones], axis=1)
        o2 = _dot(jnp.concatenate(blocks, axis=0), v_ones)
        o2 = o2[:, 0:V7X_LANES] / o2[:, V7X_LANES:2 * V7X_LANES]
        for j in range(GROUP):
            lo = o2[(2 * j) * CHUNK:(2 * j + 1) * CHUNK, :]
            hi = o2[(2 * j + 1) * CHUNK:(2 * j + 2) * CHUNK, :]
            obuf[w0 + j * CHUNK:w0 + (j + 1) * CHUNK, ls] = jnp.where(low_head, lo, hi).astype(BF16)

    bodies = [(g, p) for g in range(tm // (GROUP * CHUNK)) for p in range(d // V7X_LANES)]
    pending = {n: scores(*bodies[n]) for n in range(SCORES_AHEAD)}
    for n, body in enumerate(bodies):
        if n + SCORES_AHEAD < len(bodies):
            pending[n + SCORES_AHEAD] = scores(*bodies[n + SCORES_AHEAD])
        attend(*body, pending.pop(n))
    y = _dot(obuf[...], wo_ref[...])
    o_ref[...] = x_ref[...] + _gated_norm(y, ng_ref[1:2, :], mod_ref[2:3, :])


def _attn_first_kernel(x_ref, q_ref, k_ref, v_ref, bias_ref, wo_ref, mod_ref, ng_ref, o_ref,
                       obuf, kwin, vwin):
    tm, d = q_ref.shape
    for src, win in ((k_ref, kwin), (v_ref, vwin)):
        win[0:tm, :] = jnp.zeros((tm, d), BF16)
        win[tm:2 * tm, :] = src[...]
    _attn_tile(x_ref, q_ref, kwin, vwin, bias_ref, wo_ref, mod_ref, ng_ref, o_ref, obuf,
               first_tile=True)


def _attn_rest_kernel(x_ref, q_ref, kwin, vwin, bias_ref, wo_ref, mod_ref, ng_ref, first_ref,
                      o_ref, obuf):
    @pl.when(pl.program_id(1) == 0)
    def _():
        o_ref[...] = first_ref[...]

    @pl.when(pl.program_id(1) > 0)
    def _():
        _attn_tile(x_ref, q_ref, kwin.at[0], vwin.at[0], bias_ref, wo_ref, mod_ref, ng_ref,
                   o_ref, obuf, first_tile=False)


def _attn(x, q, k, v, bias, w_o, mod, norm_g, layer):
    bsz, s, d = x.shape
    tm = ROW_TILE
    n_tiles = s // tm
    assert tm == N_LEFT_CHUNKS * CHUNK and tm % (GROUP * CHUNK) == 0
    n_pairs = d // V7X_LANES
    scratch = [pltpu.VMEM((tm, d), BF16)]

    def call(first_tile, first_out=None):
        row_spec = pl.BlockSpec((None, tm, d), lambda b, i: (b, i, 0))
        if first_tile:
            kv_spec = row_spec
        else:
            kv_spec = pl.BlockSpec((pl.Element(1), pl.Element(2 * tm), pl.Element(d)),
                                   lambda b, i: (b, jnp.maximum(i - 1, 0) * tm, 0))
        in_specs = [
            row_spec, row_spec, kv_spec, kv_spec,
            _resident((2, n_pairs, 2 * CHUNK, SLAB), lambda b, i: (0, 0, 0, 0)),
            _resident((d, d), lambda b, i: (0, 0)),
            pl.BlockSpec((None, None, 6, d), lambda b, i: (layer, b, 0, 0)),
            _resident((None, 4, d), lambda b, i: (layer, 0, 0)),
        ]
        args = [x, q, k, v, bias, w_o, mod, norm_g]
        if first_tile:
            kv_scratch = [pltpu.VMEM((2 * tm, d), BF16), pltpu.VMEM((2 * tm, d), BF16)]
        else:
            in_specs.append(pl.BlockSpec((None, tm, d), lambda b, i: (b, 0, 0)))
            args.append(first_out)
            kv_scratch = []
        return pl.pallas_call(
            _attn_first_kernel if first_tile else _attn_rest_kernel,
            grid=(bsz, 1 if first_tile else n_tiles),
            in_specs=in_specs,
            out_specs=row_spec,
            out_shape=jax.ShapeDtypeStruct((bsz, tm if first_tile else s, d), F32),
            scratch_shapes=scratch + kv_scratch,
            compiler_params=_params(2),
            name="band_attention_first" if first_tile else "band_attention",
        )(*args)

    return call(False, call(True))


def kernel(x, c, mod_w, mod_b, norm_g, ffn_w_in, ffn_w_out, conv_w_in, conv_k, conv_w_out,
           kv_mod_w, kv_mod_b, kv_norm_g, w_kv, attn_w_q, attn_w_o, rel_bias):
    bsz, s, d = x.shape
    depth = mod_w.shape[0]
    n_a = conv_w_in.shape[0]
    n_heads = rel_bias.shape[1]
    assert d // n_heads == HEAD_DIM and s % ROW_TILE == 0

    mod = _modvec(c, mod_w, mod_b.reshape(depth, 1, 6 * d)).reshape(depth, bsz, 6, d)
    kvmod = _modvec(c, kv_mod_w[None], kv_mod_b.reshape(1, 1, 2 * d)).reshape(1, bsz, 2, d)

    conv_w_in, conv_w_out = conv_w_in.astype(BF16), conv_w_out.astype(BF16)
    kv_norm_g = kv_norm_g.reshape(1, d)
    w_kv = w_kv[None]

    k = v = None
    attn_w = None
    for layer in range(depth):
        ffn_casts = [(ffn_w_in, layer), (ffn_w_out, layer)]
        if layer < n_a:
            x, (w_in, w_out) = _mixer(x, mod, norm_g, layer, conv_w_in, conv_k, conv_w_out,
                                      layer, ffn_casts)
        else:
            bi = layer - n_a
            if attn_w is None:
                attn_w = (w_kv[0].astype(BF16), attn_w_q[bi].astype(BF16), attn_w_o[bi].astype(BF16))
            w_kv_b, w_q_b, w_o_b = attn_w
            (k_new, v_new, q), (w_in, w_out) = _proj(x, mod, kvmod, norm_g, layer, kv_norm_g,
                                                     w_kv_b, w_q_b, ffn_casts)
            if layer == n_a:
                k, v = k_new, v_new
            bias = _bias_table(rel_bias, bi).reshape(2, n_heads // 2, 2 * CHUNK, SLAB)
            x = _attn(x, q, k, v, bias, w_o_b, mod, norm_g, layer)
        nxt = layer + 1
        attn_casts = []
        if max(n_a, 1) <= nxt < depth:
            attn_casts = [(w_kv, 0), (attn_w_q, nxt - n_a), (attn_w_o, nxt - n_a)]
        x, attn_w = _ffn(x, mod, norm_g, layer, w_in, w_out, attn_casts)
        attn_w = tuple(attn_w) if attn_casts else None
    return x
```

```python
import functools
import math

import jax
import jax.numpy as jnp
from jax import lax
from jax.experimental import pallas as pl
from jax.experimental.pallas import tpu as pltpu

EPS = 1e-6
CHUNK = 64
N_LEFT_CHUNKS = 8
BAND = (N_LEFT_CHUNKS + 1) * CHUNK
SLAB = BAND + CHUNK
GROUP = 4
SCORES_AHEAD = 1
SOFTMAX_ROWS = 32
WIN = (GROUP + N_LEFT_CHUNKS) * CHUNK
HEAD_DIM = 64
MAX_REL = 2 * CHUNK
LOG2_E = math.log2(math.e)

V7X_LANES = 128
V7X_BF16_SUBLANES = 16
V7X_MXU_COLS = 256
ROW_TILE = 512
WIDE_ROW_TILE = 1024
SUB_ROWS = 256
V7X_VMEM_LIMIT = 56 * 1024 * 1024

BF16 = jnp.bfloat16
F32 = jnp.float32


def _dot(a, b):
    return jnp.dot(a, b, preferred_element_type=F32)


def _unit_rms(x):
    return x * lax.rsqrt(jnp.mean(x * x, axis=-1, keepdims=True) + EPS)


def _norm_modulate(xn, g, scale, shift):
    return xn * (g * (1.0 + scale)) + shift


def _gated_norm(y, g, gate):
    return _unit_rms(y) * (gate * g)


def _resident(block_shape, index_map):
    return pl.BlockSpec(block_shape, index_map, pipeline_mode=pl.Buffered(1))


def _params(n_grid_dims):
    return pltpu.CompilerParams(
        dimension_semantics=("arbitrary",) * n_grid_dims,
        vmem_limit_bytes=V7X_VMEM_LIMIT)


def _with_casts(body, n_in, n_out, n_cast):
    def wrapped(*refs):
        srcs = refs[n_in:n_in + n_cast]
        outs_end = n_in + n_cast + n_out
        dsts = refs[outs_end:outs_end + n_cast]
        for src, dst in zip(srcs, dsts):
            dst[...] = src[...].astype(BF16)
        body(*refs[:n_in], *refs[n_in + n_cast:outs_end], *refs[outs_end + n_cast:])
    return wrapped


def _tiled_call(body, name, grid, in_specs, args, out_specs, out_shapes, scratch=(), casts=()):
    bsz, n_tiles = grid
    c_in, c_out, c_shapes, c_args = [], [], [], []
    for w, idx in casts:
        r, c = w.shape[1:]
        rows, cols = r // n_tiles, c // bsz
        assert rows * n_tiles == r and rows % V7X_BF16_SUBLANES == 0
        assert cols * bsz == c and cols % V7X_LANES == 0
        c_in.append(pl.BlockSpec((None, rows, cols), lambda b, i, idx=idx: (idx, i, b)))
        c_out.append(pl.BlockSpec((rows, cols), lambda b, i: (i, b)))
        c_shapes.append(jax.ShapeDtypeStruct((r, c), BF16))
        c_args.append(w)
    res = pl.pallas_call(
        _with_casts(body, len(in_specs), len(out_specs), len(casts)),
        grid=grid,
        in_specs=list(in_specs) + c_in,
        out_specs=list(out_specs) + c_out,
        out_shape=list(out_shapes) + c_shapes,
        scratch_shapes=list(scratch),
        compiler_params=_params(len(grid)),
        name=name,
    )(*args, *c_args)
    return res[:len(out_specs)], res[len(out_specs):]


def _modvec_kernel(c_ref, w_ref, b_ref, o_ref):
    s = jax.nn.silu(c_ref[...]).astype(BF16)
    o_ref[...] = _dot(s, w_ref[...].astype(BF16)) + b_ref[...]


def _modvec(c, w, b, col_tile=1024):
    n_layers, d, n = w.shape
    bsz = c.shape[0]
    return pl.pallas_call(
        _modvec_kernel,
        grid=(n_layers, n // col_tile),
        in_specs=[
            pl.BlockSpec((bsz, d), lambda l, j: (0, 0)),
            pl.BlockSpec((None, d, col_tile), lambda l, j: (l, 0, j)),
            pl.BlockSpec((None, 1, col_tile), lambda l, j: (l, 0, j)),
        ],
        out_specs=pl.BlockSpec((None, bsz, col_tile), lambda l, j: (l, 0, j)),
        out_shape=jax.ShapeDtypeStruct((n_layers, bsz, n), F32),
        compiler_params=_params(2),
        name="modvec",
    )(c, w, b)


def _mixer_kernel(x_ref, mod_ref, ng_ref, win_ref, ck_ref, wout_ref, o_ref, zbuf, *, n_sub):
    tm, d = x_ref.shape
    sh1, sc1, g1 = mod_ref[0:1, :], mod_ref[1:2, :], mod_ref[2:3, :]

    @pl.when(pl.program_id(1) == 0)
    def _():
        zbuf[0:8, :] = jnp.zeros((8, d), F32)

    rows = tm // n_sub
    for r0 in range(0, tm, rows):
        x = x_ref[r0:r0 + rows, :]
        h = _norm_modulate(_unit_rms(x), ng_ref[0:1, :], sc1, sh1).astype(BF16)
        us = []
        for c0 in range(0, d, V7X_MXU_COLS):
            cs = slice(c0, c0 + V7X_MXU_COLS)
            z = (_dot(h, win_ref[:, d + c0:d + c0 + V7X_MXU_COLS])
                 * _dot(h, win_ref[:, 2 * d + c0:2 * d + c0 + V7X_MXU_COLS]))
            zbuf[8 + r0:8 + r0 + rows, cs] = z
            conv = (ck_ref[0:1, cs] * zbuf[6 + r0:6 + r0 + rows, cs]
                    + ck_ref[1:2, cs] * zbuf[7 + r0:7 + r0 + rows, cs]
                    + ck_ref[2:3, cs] * z)
            us.append((_dot(h, win_ref[:, cs]) * conv).astype(BF16))
        y = _dot(jnp.concatenate(us, axis=1), wout_ref[...])
        o_ref[r0:r0 + rows, :] = x + _gated_norm(y, ng_ref[1:2, :], g1)
    zbuf[0:8, :] = zbuf[tm:tm + 8, :]


def _mixer(x, mod, norm_g, layer, w_in, conv_k, w_out, a_idx, casts):
    bsz, s, d = x.shape
    tm = WIDE_ROW_TILE
    row_spec = pl.BlockSpec((None, tm, d), lambda b, i: (b, i, 0))
    (out,), cast = _tiled_call(
        functools.partial(_mixer_kernel, n_sub=tm // SUB_ROWS), "conv_mixer", (bsz, s // tm),
        in_specs=[
            row_spec,
            pl.BlockSpec((None, None, 6, d), lambda b, i: (layer, b, 0, 0)),
            _resident((None, 4, d), lambda b, i: (layer, 0, 0)),
            _resident((None, d, 3 * d), lambda b, i: (a_idx, 0, 0)),
            _resident((None, 3, d), lambda b, i: (a_idx, 0, 0)),
            _resident((None, d, d), lambda b, i: (a_idx, 0, 0)),
        ],
        args=(x, mod, norm_g, w_in, conv_k, w_out),
        out_specs=[row_spec],
        out_shapes=[jax.ShapeDtypeStruct(x.shape, F32)],
        scratch=[pltpu.VMEM((tm + 8, d), F32)],
        casts=casts)
    return out, cast


def _ffn_kernel(x_ref, mod_ref, ng_ref, win_ref, wout_ref, o_ref, *, n_sub):
    sh2, sc2, g2 = mod_ref[3:4, :], mod_ref[4:5, :], mod_ref[5:6, :]
    f = wout_ref.shape[0]
    rows = x_ref.shape[0] // n_sub
    for r in range(n_sub):
        rs = slice(r * rows, (r + 1) * rows)
        x = x_ref[rs, :]
        h = _norm_modulate(_unit_rms(x), ng_ref[2:3, :], sc2, sh2).astype(BF16)
        g = _dot(h, win_ref[:, 0:f])
        u = _dot(h, win_ref[:, f:2 * f])
        y = _dot((jax.nn.silu(g) * u).astype(BF16), wout_ref[...])
        o_ref[rs, :] = x + _gated_norm(y, ng_ref[3:4, :], g2)


def _ffn(x, mod, norm_g, layer, w_in, w_out, casts):
    bsz, s, d = x.shape
    f = w_out.shape[0]
    tm = WIDE_ROW_TILE
    row_spec = pl.BlockSpec((None, tm, d), lambda b, i: (b, i, 0))
    (out,), cast = _tiled_call(
        functools.partial(_ffn_kernel, n_sub=WIDE_ROW_TILE // SUB_ROWS), "swiglu_ffn", (bsz, s // tm),
        in_specs=[
            row_spec,
            pl.BlockSpec((None, None, 6, d), lambda b, i: (layer, b, 0, 0)),
            _resident((None, 4, d), lambda b, i: (layer, 0, 0)),
            _resident((d, 2 * f), lambda b, i: (0, 0)),
            _resident((f, d), lambda b, i: (0, 0)),
        ],
        args=(x, mod, norm_g, w_in, w_out),
        out_specs=[row_spec],
        out_shapes=[jax.ShapeDtypeStruct(x.shape, F32)],
        casts=casts)
    return out, cast


def _proj_kernel(x_ref, mod_ref, kvmod_ref, ng_ref, kvg_ref, wkv_ref, wq_ref,
                 k_ref, v_ref, q_ref):
    d = x_ref.shape[1]
    xn = _unit_rms(x_ref[...])
    hkv = _norm_modulate(xn, kvg_ref[...], kvmod_ref[1:2, :], kvmod_ref[0:1, :]).astype(BF16)
    k_ref[...] = _dot(hkv, wkv_ref[:, 0:d]).astype(BF16)
    v_ref[...] = _dot(hkv, wkv_ref[:, d:2 * d]).astype(BF16)
    h = _norm_modulate(xn, ng_ref[0:1, :], mod_ref[1:2, :], mod_ref[0:1, :]).astype(BF16)
    q_ref[...] = (_dot(h, wq_ref[...]) * (LOG2_E * HEAD_DIM ** -0.5)).astype(BF16)


def _proj(x, mod, kvmod, norm_g, layer, kv_norm_g, w_kv, w_q, casts):
    bsz, s, d = x.shape
    tm = ROW_TILE
    row_spec = pl.BlockSpec((None, tm, d), lambda b, i: (b, i, 0))
    out = jax.ShapeDtypeStruct(x.shape, BF16)
    return _tiled_call(
        _proj_kernel, "qkv_proj", (bsz, s // tm),
        in_specs=[
            row_spec,
            pl.BlockSpec((None, None, 6, d), lambda b, i: (layer, b, 0, 0)),
            pl.BlockSpec((None, None, 2, d), lambda b, i: (0, b, 0, 0)),
            _resident((None, 4, d), lambda b, i: (layer, 0, 0)),
            _resident((1, d), lambda b, i: (0, 0)),
            _resident((d, 2 * d), lambda b, i: (0, 0)),
            _resident((d, d), lambda b, i: (0, 0)),
        ],
        args=(x, mod, kvmod, norm_g, kv_norm_g, w_kv, w_q),
        out_specs=[row_spec, row_spec, row_spec],
        out_shapes=[out, out, out],
        casts=casts)


def _bias_kernel(rb_ref, o_ref):
    n_heads = rb_ref.shape[0]
    n_tab = 2 * MAX_REL
    rb = rb_ref[:, 0:n_tab]
    hi = rb.astype(BF16)
    r1 = rb - hi.astype(F32)
    mid = r1.astype(BF16)
    lo = (r1 - mid.astype(F32)).astype(BF16)
    pieces = jnp.concatenate([hi, mid, lo], axis=0)
    rb_top = rb_ref[:, n_tab:n_tab + 1]
    u = lax.broadcasted_iota(jnp.int32, (1, WIN), 1)
    m = jnp.where(u >= WIN - CHUNK, u - WIN, u)
    idx = jnp.clip(N_LEFT_CHUNKS * CHUNK - m, -MAX_REL, MAX_REL) + MAX_REL
    row = lax.broadcasted_iota(jnp.int32, (n_tab, WIN), 0)
    onehot = jnp.where(row == idx, 1.0, 0.0).astype(BF16)
    t3 = _dot(pieces, onehot)
    w = t3[0:n_heads] + t3[n_heads:2 * n_heads] + t3[2 * n_heads:3 * n_heads]
    w = (w + jnp.where(idx == n_tab, rb_top, 0.0)) * LOG2_E
    col = lax.broadcasted_iota(jnp.int32, (CHUNK, WIN), 1)
    neg = jnp.finfo(F32).min
    for var in range(2):
        kk = col - var * CHUNK
        in_band = (kk >= 0) & (kk < BAND)
        for h in range(n_heads):
            t = pltpu.roll(jnp.broadcast_to(w[h:h + 1, :], (CHUNK, WIN)), var * CHUNK, 1,
                           stride=1, stride_axis=0)
            o_ref[var, h] = jnp.where(in_band, t, neg)[:, 0:SLAB]


def _bias_table(rel_bias, b_idx):
    n_heads, n_rel = rel_bias.shape[1:]
    return pl.pallas_call(
        _bias_kernel,
        grid=(1,),
        in_specs=[pl.BlockSpec((None, n_heads, n_rel), lambda i: (b_idx, 0, 0))],
        out_specs=pl.BlockSpec((2, n_heads, CHUNK, SLAB), lambda i: (0, 0, 0, 0)),
        out_shape=jax.ShapeDtypeStruct((2, n_heads, CHUNK, SLAB), F32),
        compiler_params=_params(1),
        name="rel_bias_table",
    )(rel_bias)


def _attn_tile(x_ref, q_ref, kwin, vwin, bias_ref, wo_ref, mod_ref, ng_ref, o_ref, obuf,
               *, first_tile):
    tm, d = q_ref.shape
    m_rows = 2 * GROUP * CHUNK
    low_head = lax.broadcasted_iota(jnp.int32, (CHUNK, V7X_LANES), 1) < HEAD_DIM
    neg = jnp.finfo(F32).min
    ones = jnp.ones((WIN, V7X_LANES), BF16)
    zcols = jnp.zeros((SOFTMAX_ROWS, WIN - SLAB), BF16)
    col = lax.broadcasted_iota(jnp.int32, (m_rows, WIN), 1) if first_tile else None

    def scores(g, p):
        w0 = g * GROUP * CHUNK
        ls = slice(p * V7X_LANES, (p + 1) * V7X_LANES)
        q2 = []
        for j in range(GROUP):
            qp = q_ref[w0 + j * CHUNK:w0 + (j + 1) * CHUNK, ls]
            zero = jnp.zeros_like(qp)
            q2 += [jnp.where(low_head, qp, zero), jnp.where(low_head, zero, qp)]
        s = lax.dot_general(jnp.concatenate(q2, axis=0), kwin[w0:w0 + WIN, ls],
                            (((1,), (1,)), ((), ())), preferred_element_type=F32)
        if first_tile:
            s = jnp.where(col < tm - w0, neg, s)
        return s

    def attend(g, p, s):
        w0 = g * GROUP * CHUNK
        ls = slice(p * V7X_LANES, (p + 1) * V7X_LANES)
        blocks = []
        for j in range(GROUP):
            t0 = (j // 2) * V7X_LANES
            for b0 in range(0, 2 * CHUNK, SOFTMAX_ROWS):
                r0 = 2 * j * CHUNK + b0
                sl = (s[r0:r0 + SOFTMAX_ROWS, t0:t0 + SLAB]
                      + bias_ref[j % 2, p, b0:b0 + SOFTMAX_ROWS, :])
                ex = jnp.exp2(sl - jnp.max(sl, axis=-1, keepdims=True)).astype(BF16)
                blocks.append(jnp.concatenate([ex, zcols] if t0 == 0 else [zcols, ex], axis=1))
        v_ones = jnp.concatenate([vwin[w0:w0 + WIN, ls], ones], axis=1)
        o2 = _dot(jnp.concatenate(blocks, axis=0), v_ones)
        o2 = o2[:, 0:V7X_LANES] / o2[:, V7X_LANES:2 * V7X_LANES]
        for j in range(GROUP):
            lo = o2[(2 * j) * CHUNK:(2 * j + 1) * CHUNK, :]
            hi = o2[(2 * j + 1) * CHUNK:(2 * j + 2) * CHUNK, :]
            obuf[w0 + j * CHUNK:w0 + (j + 1) * CHUNK, ls] = jnp.where(low_head, lo, hi).astype(BF16)

    bodies = [(g, p) for g in range(tm // (GROUP * CHUNK)) for p in range(d // V7X_LANES)]
    pending = {n: scores(*bodies[n]) for n in range(SCORES_AHEAD)}
    for n, body in enumerate(bodies):
        if n + SCORES_AHEAD < len(bodies):
            pending[n + SCORES_AHEAD] = scores(*bodies[n + SCORES_AHEAD])
        attend(*body, pending.pop(n))
    y = _dot(obuf[...], wo_ref[...])
    o_ref[...] = x_ref[...] + _gated_norm(y, ng_ref[1:2, :], mod_ref[2:3, :])


def _attn_first_kernel(x_ref, q_ref, k_ref, v_ref, bias_ref, wo_ref, mod_ref, ng_ref, o_ref,
                       obuf, kwin, vwin):
    tm, d = q_ref.shape
    for src, win in ((k_ref, kwin), (v_ref, vwin)):
        win[0:tm, :] = jnp.zeros((tm, d), BF16)
        win[tm:2 * tm, :] = src[...]
    _attn_tile(x_ref, q_ref, kwin, vwin, bias_ref, wo_ref, mod_ref, ng_ref, o_ref, obuf,
               first_tile=True)


def _attn_rest_kernel(x_ref, q_ref, kwin, vwin, bias_ref, wo_ref, mod_ref, ng_ref, first_ref,
                      o_ref, obuf):
    @pl.when(pl.program_id(1) == 0)
    def _():
        o_ref[...] = first_ref[...]

    @pl.when(pl.program_id(1) > 0)
    def _():
        _attn_tile(x_ref, q_ref, kwin.at[0], vwin.at[0], bias_ref, wo_ref, mod_ref, ng_ref,
                   o_ref, obuf, first_tile=False)


def _attn(x, q, k, v, bias, w_o, mod, norm_g, layer):
    bsz, s, d = x.shape
    tm = ROW_TILE
    n_tiles = s // tm
    assert tm == N_LEFT_CHUNKS * CHUNK and tm % (GROUP * CHUNK) == 0
    n_pairs = d // V7X_LANES
    scratch = [pltpu.VMEM((tm, d), BF16)]

    def call(first_tile, first_out=None):
        row_spec = pl.BlockSpec((None, tm, d), lambda b, i: (b, i, 0))
        if first_tile:
            kv_spec = row_spec
        else:
            kv_spec = pl.BlockSpec((pl.Element(1), pl.Element(2 * tm), pl.Element(d)),
                                   lambda b, i: (b, jnp.maximum(i - 1, 0) * tm, 0))
        in_specs = [
            row_spec, row_spec, kv_spec, kv_spec,
            _resident((2, n_pairs, 2 * CHUNK, SLAB), lambda b, i: (0, 0, 0, 0)),
            _resident((d, d), lambda b, i: (0, 0)),
            pl.BlockSpec((None, None, 6, d), lambda b, i: (layer, b, 0, 0)),
            _resident((None, 4, d), lambda b, i: (layer, 0, 0)),
        ]
        args = [x, q, k, v, bias, w_o, mod, norm_g]
        if first_tile:
            kv_scratch = [pltpu.VMEM((2 * tm, d), BF16), pltpu.VMEM((2 * tm, d), BF16)]
        else:
            in_specs.append(pl.BlockSpec((None, tm, d), lambda b, i: (b, 0, 0)))
            args.append(first_out)
            kv_scratch = []
        return pl.pallas_call(
            _attn_first_kernel if first_tile else _attn_rest_kernel,
            grid=(bsz, 1 if first_tile else n_tiles),
            in_specs=in_specs,
            out_specs=row_spec,
            out_shape=jax.ShapeDtypeStruct((bsz, tm if first_tile else s, d), F32),
            scratch_shapes=scratch + kv_scratch,
            compiler_params=_params(2),
            name="band_attention_first" if first_tile else "band_attention",
        )(*args)

    return call(False, call(True))


def kernel(x, c, mod_w, mod_b, norm_g, ffn_w_in, ffn_w_out, conv_w_in, conv_k, conv_w_out,
           kv_mod_w, kv_mod_b, kv_norm_g, w_kv, attn_w_q, attn_w_o, rel_bias):
    bsz, s, d = x.shape
    depth = mod_w.shape[0]
    n_a = conv_w_in.shape[0]
    n_heads = rel_bias.shape[1]
    assert d // n_heads == HEAD_DIM and s % WIDE_ROW_TILE == 0 and WIDE_ROW_TILE % ROW_TILE == 0

    mod = _modvec(c, mod_w, mod_b.reshape(depth, 1, 6 * d)).reshape(depth, bsz, 6, d)
    kvmod = _modvec(c, kv_mod_w[None], kv_mod_b.reshape(1, 1, 2 * d)).reshape(1, bsz, 2, d)

    conv_w_in, conv_w_out = conv_w_in.astype(BF16), conv_w_out.astype(BF16)
    kv_norm_g = kv_norm_g.reshape(1, d)
    w_kv = w_kv[None]

    k = v = None
    attn_w = None
    for layer in range(depth):
        ffn_casts = [(ffn_w_in, layer), (ffn_w_out, layer)]
        if layer < n_a:
            x, (w_in, w_out) = _mixer(x, mod, norm_g, layer, conv_w_in, conv_k, conv_w_out,
                                      layer, ffn_casts)
        else:
            bi = layer - n_a
            if attn_w is None:
                attn_w = (w_kv[0].astype(BF16), attn_w_q[bi].astype(BF16), attn_w_o[bi].astype(BF16))
            w_kv_b, w_q_b, w_o_b = attn_w
            (k_new, v_new, q), (w_in, w_out) = _proj(x, mod, kvmod, norm_g, layer, kv_norm_g,
                                                     w_kv_b, w_q_b, ffn_casts)
            if layer == n_a:
                k, v = k_new, v_new
            bias = _bias_table(rel_bias, bi).reshape(2, n_heads // 2, 2 * CHUNK, SLAB)
            x = _attn(x, q, k, v, bias, w_o_b, mod, norm_g, layer)
        nxt = layer + 1
        attn_casts = []
        if max(n_a, 1) <= nxt < depth:
            attn_casts = [(w_kv, 0), (attn_w_q, nxt - n_a), (attn_w_o, nxt - n_a)]
        x, attn_w = _ffn(x, mod, norm_g, layer, w_in, w_out, attn_casts)
        attn_w = tuple(attn_w) if attn_casts else None
    return x
```

```python
import functools
import math

import jax
import jax.numpy as jnp
from jax import lax
from jax.experimental import pallas as pl
from jax.experimental.pallas import tpu as pltpu

EPS = 1e-6
CHUNK = 64
N_LEFT_CHUNKS = 8
BAND = (N_LEFT_CHUNKS + 1) * CHUNK
SLAB = BAND + CHUNK
GROUP = 4
SCORES_AHEAD = 1
SOFTMAX_ROWS = 32
WIN = (GROUP + N_LEFT_CHUNKS) * CHUNK
HEAD_DIM = 64
MAX_REL = 2 * CHUNK
LOG2_E = math.log2(math.e)

V7X_LANES = 128
V7X_BF16_SUBLANES = 16
V7X_MXU_COLS = 256
ROW_TILE = 512
WIDE_ROW_TILE = 1024
SUB_ROWS = 256
V7X_VMEM_LIMIT = 56 * 1024 * 1024

BF16 = jnp.bfloat16
F32 = jnp.float32


def _dot(a, b):
    return jnp.dot(a, b, preferred_element_type=F32)


def _unit_rms(x):
    return x * lax.rsqrt(jnp.mean(x * x, axis=-1, keepdims=True) + EPS)


def _norm_modulate(xn, g, scale, shift):
    return xn * (g * (1.0 + scale)) + shift


def _gated_norm(y, g, gate):
    return _unit_rms(y) * (gate * g)


def _resident(block_shape, index_map):
    return pl.BlockSpec(block_shape, index_map, pipeline_mode=pl.Buffered(1))


def _params(n_grid_dims):
    return pltpu.CompilerParams(
        dimension_semantics=("arbitrary",) * n_grid_dims,
        vmem_limit_bytes=V7X_VMEM_LIMIT)


def _with_casts(body, n_in, n_out, n_cast):
    def wrapped(*refs):
        srcs = refs[n_in:n_in + n_cast]
        outs_end = n_in + n_cast + n_out
        dsts = refs[outs_end:outs_end + n_cast]
        for src, dst in zip(srcs, dsts):
            dst[...] = src[...].astype(BF16)
        body(*refs[:n_in], *refs[n_in + n_cast:outs_end], *refs[outs_end + n_cast:])
    return wrapped


def _tiled_call(body, name, grid, in_specs, args, out_specs, out_shapes, scratch=(), casts=()):
    bsz, n_tiles = grid
    c_in, c_out, c_shapes, c_args = [], [], [], []
    for w, idx in casts:
        r, c = w.shape[1:]
        rows, cols = r // n_tiles, c // bsz
        assert rows * n_tiles == r and rows % V7X_BF16_SUBLANES == 0
        assert cols * bsz == c and cols % V7X_LANES == 0
        c_in.append(pl.BlockSpec((None, rows, cols), lambda b, i, idx=idx: (idx, i, b)))
        c_out.append(pl.BlockSpec((rows, cols), lambda b, i: (i, b)))
        c_shapes.append(jax.ShapeDtypeStruct((r, c), BF16))
        c_args.append(w)
    res = pl.pallas_call(
        _with_casts(body, len(in_specs), len(out_specs), len(casts)),
        grid=grid,
        in_specs=list(in_specs) + c_in,
        out_specs=list(out_specs) + c_out,
        out_shape=list(out_shapes) + c_shapes,
        scratch_shapes=list(scratch),
        compiler_params=_params(len(grid)),
        name=name,
    )(*args, *c_args)
    return res[:len(out_specs)], res[len(out_specs):]


def _modvec_kernel(c_ref, w_ref, b_ref, o_ref):
    s = jax.nn.silu(c_ref[...]).astype(BF16)
    o_ref[...] = _dot(s, w_ref[...].astype(BF16)) + b_ref[...]


def _modvec(c, w, b, col_tile=1024):
    n_layers, d, n = w.shape
    bsz = c.shape[0]
    return pl.pallas_call(
        _modvec_kernel,
        grid=(n_layers, n // col_tile),
        in_specs=[
            pl.BlockSpec((bsz, d), lambda l, j: (0, 0)),
            pl.BlockSpec((None, d, col_tile), lambda l, j: (l, 0, j)),
            pl.BlockSpec((None, 1, col_tile), lambda l, j: (l, 0, j)),
        ],
        out_specs=pl.BlockSpec((None, bsz, col_tile), lambda l, j: (l, 0, j)),
        out_shape=jax.ShapeDtypeStruct((n_layers, bsz, n), F32),
        compiler_params=_params(2),
        name="modvec",
    )(c, w, b)


def _mixer_kernel(x_ref, mod_ref, ng_ref, win_ref, ck_ref, wout_ref, o_ref, zbuf, *, n_sub):
    tm, d = x_ref.shape
    sh1, sc1, g1 = mod_ref[0:1, :], mod_ref[1:2, :], mod_ref[2:3, :]

    @pl.when(pl.program_id(1) == 0)
    def _():
        zbuf[0:8, :] = jnp.zeros((8, d), F32)

    rows = tm // n_sub
    for r0 in range(0, tm, rows):
        x = x_ref[r0:r0 + rows, :]
        h = _norm_modulate(_unit_rms(x), ng_ref[0:1, :], sc1, sh1).astype(BF16)
        us = []
        for c0 in range(0, d, V7X_MXU_COLS):
            cs = slice(c0, c0 + V7X_MXU_COLS)
            z = (_dot(h, win_ref[:, d + c0:d + c0 + V7X_MXU_COLS])
                 * _dot(h, win_ref[:, 2 * d + c0:2 * d + c0 + V7X_MXU_COLS]))
            zbuf[8 + r0:8 + r0 + rows, cs] = z
            conv = (ck_ref[0:1, cs] * zbuf[6 + r0:6 + r0 + rows, cs]
                    + ck_ref[1:2, cs] * zbuf[7 + r0:7 + r0 + rows, cs]
                    + ck_ref[2:3, cs] * z)
            us.append((_dot(h, win_ref[:, cs]) * conv).astype(BF16))
        y = _dot(jnp.concatenate(us, axis=1), wout_ref[...])
        o_ref[r0:r0 + rows, :] = x + _gated_norm(y, ng_ref[1:2, :], g1)
    zbuf[0:8, :] = zbuf[tm:tm + 8, :]


def _mixer(x, mod, norm_g, layer, w_in, conv_k, w_out, a_idx, casts):
    bsz, s, d = x.shape
    tm = WIDE_ROW_TILE
    row_spec = pl.BlockSpec((None, tm, d), lambda b, i: (b, i, 0))
    (out,), cast = _tiled_call(
        functools.partial(_mixer_kernel, n_sub=tm // SUB_ROWS), "conv_mixer", (bsz, s // tm),
        in_specs=[
            row_spec,
            pl.BlockSpec((None, None, 6, d), lambda b, i: (layer, b, 0, 0)),
            _resident((None, 4, d), lambda b, i: (layer, 0, 0)),
            _resident((None, d, 3 * d), lambda b, i: (a_idx, 0, 0)),
            _resident((None, 3, d), lambda b, i: (a_idx, 0, 0)),
            _resident((None, d, d), lambda b, i: (a_idx, 0, 0)),
        ],
        args=(x, mod, norm_g, w_in, conv_k, w_out),
        out_specs=[row_spec],
        out_shapes=[jax.ShapeDtypeStruct(x.shape, F32)],
        scratch=[pltpu.VMEM((tm + 8, d), F32)],
        casts=casts)
    return out, cast


def _ffn_kernel(x_ref, mod_ref, ng_ref, win_ref, wout_ref, o_ref, *, n_sub):
    sh2, sc2, g2 = mod_ref[3:4, :], mod_ref[4:5, :], mod_ref[5:6, :]
    f = wout_ref.shape[0]
    rows = x_ref.shape[0] // n_sub
    for r in range(n_sub):
        rs = slice(r * rows, (r + 1) * rows)
        x = x_ref[rs, :]
        h = _norm_modulate(_unit_rms(x), ng_ref[2:3, :], sc2, sh2).astype(BF16)
        g = _dot(h, win_ref[:, 0:f])
        u = _dot(h, win_ref[:, f:2 * f])
        y = _dot((jax.nn.silu(g) * u).astype(BF16), wout_ref[...])
        o_ref[rs, :] = x + _gated_norm(y, ng_ref[3:4, :], g2)


def _ffn(x, mod, norm_g, layer, w_in, w_out, casts):
    bsz, s, d = x.shape
    f = w_out.shape[0]
    tm = WIDE_ROW_TILE
    row_spec = pl.BlockSpec((None, tm, d), lambda b, i: (b, i, 0))
    (out,), cast = _tiled_call(
        functools.partial(_ffn_kernel, n_sub=WIDE_ROW_TILE // SUB_ROWS), "swiglu_ffn", (bsz, s // tm),
        in_specs=[
            row_spec,
            pl.BlockSpec((None, None, 6, d), lambda b, i: (layer, b, 0, 0)),
            _resident((None, 4, d), lambda b, i: (layer, 0, 0)),
            _resident((d, 2 * f), lambda b, i: (0, 0)),
            _resident((f, d), lambda b, i: (0, 0)),
        ],
        args=(x, mod, norm_g, w_in, w_out),
        out_specs=[row_spec],
        out_shapes=[jax.ShapeDtypeStruct(x.shape, F32)],
        casts=casts)
    return out, cast


def _proj_kernel(x_ref, mod_ref, kvmod_ref, ng_ref, kvg_ref, wkv_ref, wq_ref,
                 k_ref, v_ref, q_ref):
    d = x_ref.shape[1]
    xn = _unit_rms(x_ref[...])
    hkv = _norm_modulate(xn, kvg_ref[...], kvmod_ref[1:2, :], kvmod_ref[0:1, :]).astype(BF16)
    k_ref[...] = _dot(hkv, wkv_ref[:, 0:d]).astype(BF16)
    v_ref[...] = _dot(hkv, wkv_ref[:, d:2 * d]).astype(BF16)
    h = _norm_modulate(xn, ng_ref[0:1, :], mod_ref[1:2, :], mod_ref[0:1, :]).astype(BF16)
    q_ref[...] = (_dot(h, wq_ref[...]) * (LOG2_E * HEAD_DIM ** -0.5)).astype(BF16)


def _proj(x, mod, kvmod, norm_g, layer, kv_norm_g, w_kv, w_q, casts):
    bsz, s, d = x.shape
    tm = ROW_TILE
    row_spec = pl.BlockSpec((None, tm, d), lambda b, i: (b, i, 0))
    out = jax.ShapeDtypeStruct(x.shape, BF16)
    return _tiled_call(
        _proj_kernel, "qkv_proj", (bsz, s // tm),
        in_specs=[
            row_spec,
            pl.BlockSpec((None, None, 6, d), lambda b, i: (layer, b, 0, 0)),
            pl.BlockSpec((None, None, 2, d), lambda b, i: (0, b, 0, 0)),
            _resident((None, 4, d), lambda b, i: (layer, 0, 0)),
            _resident((1, d), lambda b, i: (0, 0)),
            _resident((d, 2 * d), lambda b, i: (0, 0)),
            _resident((d, d), lambda b, i: (0, 0)),
        ],
        args=(x, mod, kvmod, norm_g, kv_norm_g, w_kv, w_q),
        out_specs=[row_spec, row_spec, row_spec],
        out_shapes=[out, out, out],
        casts=casts)


def _bias_kernel(rb_ref, o_ref):
    n_heads = rb_ref.shape[0]
    n_tab = 2 * MAX_REL
    rb = rb_ref[:, 0:n_tab]
    hi = rb.astype(BF16)
    r1 = rb - hi.astype(F32)
    mid = r1.astype(BF16)
    lo = (r1 - mid.astype(F32)).astype(BF16)
    pieces = jnp.concatenate([hi, mid, lo], axis=0)
    rb_top = rb_ref[:, n_tab:n_tab + 1]
    u = lax.broadcasted_iota(jnp.int32, (1, WIN), 1)
    m = jnp.where(u >= WIN - CHUNK, u - WIN, u)
    idx = jnp.clip(N_LEFT_CHUNKS * CHUNK - m, -MAX_REL, MAX_REL) + MAX_REL
    row = lax.broadcasted_iota(jnp.int32, (n_tab, WIN), 0)
    onehot = jnp.where(row == idx, 1.0, 0.0).astype(BF16)
    t3 = _dot(pieces, onehot)
    w = t3[0:n_heads] + t3[n_heads:2 * n_heads] + t3[2 * n_heads:3 * n_heads]
    w = (w + jnp.where(idx == n_tab, rb_top, 0.0)) * LOG2_E
    col = lax.broadcasted_iota(jnp.int32, (CHUNK, WIN), 1)
    neg = jnp.finfo(F32).min
    for var in range(2):
        kk = col - var * CHUNK
        in_band = (kk >= 0) & (kk < BAND)
        for h in range(n_heads):
            t = pltpu.roll(jnp.broadcast_to(w[h:h + 1, :], (CHUNK, WIN)), var * CHUNK, 1,
                           stride=1, stride_axis=0)
            o_ref[var, h] = jnp.where(in_band, t, neg)[:, 0:SLAB]


def _bias_table(rel_bias, b_idx):
    n_heads, n_rel = rel_bias.shape[1:]
    return pl.pallas_call(
        _bias_kernel,
        grid=(1,),
        in_specs=[pl.BlockSpec((None, n_heads, n_rel), lambda i: (b_idx, 0, 0))],
        out_specs=pl.BlockSpec((2, n_heads, CHUNK, SLAB), lambda i: (0, 0, 0, 0)),
        out_shape=jax.ShapeDtypeStruct((2, n_heads, CHUNK, SLAB), F32),
        compiler_params=_params(1),
        name="rel_bias_table",
    )(rel_bias)


def _attn_tile(x_ref, q_ref, kwin, vwin, bias_ref, wo_ref, mod_ref, ng_ref, o_ref, obuf,
               *, first_tile):
    tm, d = q_ref.shape
    m_rows = 2 * GROUP * CHUNK
    low_head = lax.broadcasted_iota(jnp.int32, (CHUNK, V7X_LANES), 1) < HEAD_DIM
    neg = jnp.finfo(F32).min
    ones = jnp.ones((WIN, V7X_LANES), BF16)
    zcols = jnp.zeros((SOFTMAX_ROWS, WIN - SLAB), BF16)
    col = lax.broadcasted_iota(jnp.int32, (m_rows, WIN), 1) if first_tile else None

    def scores(g, p):
        w0 = g * GROUP * CHUNK
        ls = slice(p * V7X_LANES, (p + 1) * V7X_LANES)
        q2 = []
        for j in range(GROUP):
            qp = q_ref[w0 + j * CHUNK:w0 + (j + 1) * CHUNK, ls]
            zero = jnp.zeros_like(qp)
            q2 += [jnp.where(low_head, qp, zero), jnp.where(low_head, zero, qp)]
        s = lax.dot_general(jnp.concatenate(q2, axis=0), kwin[w0:w0 + WIN, ls],
                            (((1,), (1,)), ((), ())), preferred_element_type=F32)
        if first_tile:
            s = jnp.where(col < tm - w0, neg, s)
        return s

    def attend(g, p, s):
        w0 = g * GROUP * CHUNK
        ls = slice(p * V7X_LANES, (p + 1) * V7X_LANES)
        blocks = []
        for j in range(GROUP):
            t0 = (j // 2) * V7X_LANES
            for b0 in range(0, 2 * CHUNK, SOFTMAX_ROWS):
                r0 = 2 * j * CHUNK + b0
                sl = (s[r0:r0 + SOFTMAX_ROWS, t0:t0 + SLAB]
                      + bias_ref[j % 2, p, b0:b0 + SOFTMAX_ROWS, :])
                ex = jnp.exp2(sl - jnp.max(sl, axis=-1, keepdims=True)).astype(BF16)
                blocks.append(jnp.concatenate([ex, zcols] if t0 == 0 else [zcols, ex], axis=1))
        v_ones = jnp.concatenate([vwin[w0:w0 + WIN, ls], ones], axis=1)
        o2 = _dot(jnp.concatenate(blocks, axis=0), v_ones)
        o2 = o2[:, 0:V7X_LANES] / o2[:, V7X_LANES:2 * V7X_LANES]
        for j in range(GROUP):
            lo = o2[(2 * j) * CHUNK:(2 * j + 1) * CHUNK, :]
            hi = o2[(2 * j + 1) * CHUNK:(2 * j + 2) * CHUNK, :]
            obuf[w0 + j * CHUNK:w0 + (j + 1) * CHUNK, ls] = jnp.where(low_head, lo, hi).astype(BF16)

    bodies = [(g, p) for g in range(tm // (GROUP * CHUNK)) for p in range(d // V7X_LANES)]
    pending = {n: scores(*bodies[n]) for n in range(SCORES_AHEAD)}
    for n, body in enumerate(bodies):
        if n + SCORES_AHEAD < len(bodies):
            pending[n + SCORES_AHEAD] = scores(*bodies[n + SCORES_AHEAD])
        attend(*body, pending.pop(n))
    y = _dot(obuf[...], wo_ref[...])
    o_ref[...] = x_ref[...] + _gated_norm(y, ng_ref[1:2, :], mod_ref[2:3, :])


def _attn_first_kernel(x_ref, q_ref, k_ref, v_ref, bias_ref, wo_ref, mod_ref, ng_ref, o_ref,
                       obuf, kwin, vwin):
    tm, d = q_ref.shape
    for src, win in ((k_ref, kwin), (v_ref, vwin)):
        win[0:tm, :] = jnp.zeros((tm, d), BF16)
        win[tm:2 * tm, :] = src[...]
    _attn_tile(x_ref, q_ref, kwin, vwin, bias_ref, wo_ref, mod_ref, ng_ref, o_ref, obuf,
               first_tile=True)


def _attn_rest_kernel(x_ref, q_ref, kwin, vwin, bias_ref, wo_ref, mod_ref, ng_ref, first_ref,
                      o_ref, obuf):
    @pl.when(pl.program_id(1) == 0)
    def _():
        o_ref[...] = first_ref[...]

    @pl.when(pl.program_id(1) > 0)
    def _():
        _attn_tile(x_ref, q_ref, kwin.at[0], vwin.at[0], bias_ref, wo_ref, mod_ref, ng_ref,
                   o_ref, obuf, first_tile=False)


def _attn(x, q, k, v, bias, w_o, mod, norm_g, layer):
    bsz, s, d = x.shape
    tm = ROW_TILE
    n_tiles = s // tm
    assert tm == N_LEFT_CHUNKS * CHUNK and tm % (GROUP * CHUNK) == 0
    n_pairs = d // V7X_LANES
    scratch = [pltpu.VMEM((tm, d), BF16)]

    def call(first_tile, first_out=None):
        row_spec = pl.BlockSpec((None, tm, d), lambda b, i: (b, i, 0))
        if first_tile:
            kv_spec = row_spec
        else:
            kv_spec = pl.BlockSpec((pl.Element(1), pl.Element(2 * tm), pl.Element(d)),
                                   lambda b, i: (b, jnp.maximum(i - 1, 0) * tm, 0))
        in_specs = [
            row_spec, row_spec, kv_spec, kv_spec,
            _resident((2, n_pairs, 2 * CHUNK, SLAB), lambda b, i: (0, 0, 0, 0)),
            _resident((d, d), lambda b, i: (0, 0)),
            pl.BlockSpec((None, None, 6, d), lambda b, i: (layer, b, 0, 0)),
            _resident((None, 4, d), lambda b, i: (layer, 0, 0)),
        ]
        args = [x, q, k, v, bias, w_o, mod, norm_g]
        if first_tile:
            kv_scratch = [pltpu.VMEM((2 * tm, d), BF16), pltpu.VMEM((2 * tm, d), BF16)]
        else:
            in_specs.append(pl.BlockSpec((None, tm, d), lambda b, i: (b, 0, 0)))
            args.append(first_out)
            kv_scratch = []
        return pl.pallas_call(
            _attn_first_kernel if first_tile else _attn_rest_kernel,
            grid=(bsz, 1 if first_tile else n_tiles),
            in_specs=in_specs,
            out_specs=row_spec,
            out_shape=jax.ShapeDtypeStruct((bsz, tm if first_tile else s, d), F32),
            scratch_shapes=scratch + kv_scratch,
            compiler_params=_params(2),
            name="band_attention_first" if first_tile else "band_attention",
        )(*args)

    return call(False, call(True))


def kernel(x, c, mod_w, mod_b, norm_g, ffn_w_in, ffn_w_out, conv_w_in, conv_k, conv_w_out,
           kv_mod_w, kv_mod_b, kv_norm_g, w_kv, attn_w_q, attn_w_o, rel_bias):
    bsz, s, d = x.shape
    depth = mod_w.shape[0]
    n_a = conv_w_in.shape[0]
    n_heads = rel_bias.shape[1]
    assert d // n_heads == HEAD_DIM and s % WIDE_ROW_TILE == 0 and WIDE_ROW_TILE % ROW_TILE == 0

    mod = _modvec(c, mod_w, mod_b.reshape(depth, 1, 6 * d)).reshape(depth, bsz, 6, d)
    kvmod = _modvec(c, kv_mod_w[None], kv_mod_b.reshape(1, 1, 2 * d)).reshape(1, bsz, 2, d)

    conv_w_in, conv_w_out = conv_w_in.astype(BF16), conv_w_out.astype(BF16)
    kv_norm_g = kv_norm_g.reshape(1, d)
    w_kv = w_kv[None]

    k = v = None
    attn_w = None
    ffn_w = None
    for layer in range(depth):
        ffn_casts = [] if ffn_w else [(ffn_w_in, layer), (ffn_w_out, layer)]
        if layer < n_a:
            x, cast = _mixer(x, mod, norm_g, layer, conv_w_in, conv_k, conv_w_out, layer, ffn_casts)
        else:
            bi = layer - n_a
            if attn_w is None:
                attn_w = (w_kv[0].astype(BF16), attn_w_q[bi].astype(BF16), attn_w_o[bi].astype(BF16))
            w_kv_b, w_q_b, w_o_b = attn_w
            (k_new, v_new, q), cast = _proj(x, mod, kvmod, norm_g, layer, kv_norm_g,
                                            w_kv_b, w_q_b, ffn_casts)
            if layer == n_a:
                k, v = k_new, v_new
            bias = _bias_table(rel_bias, bi).reshape(2, n_heads // 2, 2 * CHUNK, SLAB)
            x = _attn(x, q, k, v, bias, w_o_b, mod, norm_g, layer)
        w_in, w_out = ffn_w or cast
        nxt = layer + 1
        next_casts = []
        if nxt < depth:
            next_casts = [(ffn_w_in, nxt), (ffn_w_out, nxt)]
            if nxt >= n_a:
                next_casts += [(w_kv, 0), (attn_w_q, nxt - n_a), (attn_w_o, nxt - n_a)]
        x, cast = _ffn(x, mod, norm_g, layer, w_in, w_out, next_casts)
        ffn_w = tuple(cast[0:2]) if nxt < depth else None
        attn_w = tuple(cast[2:5]) if len(cast) == 5 else None
    return x
```

```python
import functools
import math

import jax
import jax.numpy as jnp
from jax import lax
from jax.experimental import pallas as pl
from jax.experimental.pallas import tpu as pltpu

EPS = 1e-6
CHUNK = 64
N_LEFT_CHUNKS = 8
LEFT_ROWS = N_LEFT_CHUNKS * CHUNK
BAND = (N_LEFT_CHUNKS + 1) * CHUNK
SLAB = BAND + CHUNK
GROUP = 4
SCORES_AHEAD = 1
SOFTMAX_ROWS = 32
WIN = (GROUP + N_LEFT_CHUNKS) * CHUNK
HEAD_DIM = 64
MAX_REL = 2 * CHUNK
LOG2_E = math.log2(math.e)

V7X_LANES = 128
V7X_BF16_SUBLANES = 16
V7X_MXU_COLS = 256
ROW_TILE = 512
WIDE_ROW_TILE = 1024
SUB_ROWS = 256
V7X_VMEM_LIMIT = 56 * 1024 * 1024

BF16 = jnp.bfloat16
F32 = jnp.float32


def _dot(a, b):
    return jnp.dot(a, b, preferred_element_type=F32)


def _unit_rms(x):
    return x * lax.rsqrt(jnp.mean(x * x, axis=-1, keepdims=True) + EPS)


def _norm_modulate(xn, g, scale, shift):
    return xn * (g * (1.0 + scale)) + shift


def _gated_norm(y, g, gate):
    return _unit_rms(y) * (gate * g)


def _resident(block_shape, index_map):
    return pl.BlockSpec(block_shape, index_map, pipeline_mode=pl.Buffered(1))


def _params(n_grid_dims):
    return pltpu.CompilerParams(
        dimension_semantics=("arbitrary",) * n_grid_dims,
        vmem_limit_bytes=V7X_VMEM_LIMIT)


def _with_casts(body, n_in, n_out, n_cast):
    def wrapped(*refs):
        srcs = refs[n_in:n_in + n_cast]
        outs_end = n_in + n_cast + n_out
        dsts = refs[outs_end:outs_end + n_cast]
        for src, dst in zip(srcs, dsts):
            dst[...] = src[...].astype(BF16)
        body(*refs[:n_in], *refs[n_in + n_cast:outs_end], *refs[outs_end + n_cast:])
    return wrapped


def _tiled_call(body, name, grid, in_specs, args, out_specs, out_shapes, scratch=(), casts=()):
    bsz, n_tiles = grid
    c_in, c_out, c_shapes, c_args = [], [], [], []
    for w, idx in casts:
        r, c = w.shape[1:]
        rows, cols = r // n_tiles, c // bsz
        assert rows * n_tiles == r and rows % V7X_BF16_SUBLANES == 0
        assert cols * bsz == c and cols % V7X_LANES == 0
        c_in.append(pl.BlockSpec((None, rows, cols), lambda b, i, idx=idx: (idx, i, b)))
        c_out.append(pl.BlockSpec((rows, cols), lambda b, i: (i, b)))
        c_shapes.append(jax.ShapeDtypeStruct((r, c), BF16))
        c_args.append(w)
    res = pl.pallas_call(
        _with_casts(body, len(in_specs), len(out_specs), len(casts)),
        grid=grid,
        in_specs=list(in_specs) + c_in,
        out_specs=list(out_specs) + c_out,
        out_shape=list(out_shapes) + c_shapes,
        scratch_shapes=list(scratch),
        compiler_params=_params(len(grid)),
        name=name,
    )(*args, *c_args)
    return res[:len(out_specs)], res[len(out_specs):]


def _modvec_kernel(c_ref, w_ref, b_ref, o_ref):
    s = jax.nn.silu(c_ref[...]).astype(BF16)
    o_ref[...] = _dot(s, w_ref[...].astype(BF16)) + b_ref[...]


def _modvec(c, w, b, col_tile=1024):
    n_layers, d, n = w.shape
    bsz = c.shape[0]
    return pl.pallas_call(
        _modvec_kernel,
        grid=(n_layers, n // col_tile),
        in_specs=[
            pl.BlockSpec((bsz, d), lambda l, j: (0, 0)),
            pl.BlockSpec((None, d, col_tile), lambda l, j: (l, 0, j)),
            pl.BlockSpec((None, 1, col_tile), lambda l, j: (l, 0, j)),
        ],
        out_specs=pl.BlockSpec((None, bsz, col_tile), lambda l, j: (l, 0, j)),
        out_shape=jax.ShapeDtypeStruct((n_layers, bsz, n), F32),
        compiler_params=_params(2),
        name="modvec",
    )(c, w, b)


def _mixer_kernel(x_ref, mod_ref, ng_ref, win_ref, ck_ref, wout_ref, o_ref, zbuf, *, n_sub):
    tm, d = x_ref.shape
    sh1, sc1, g1 = mod_ref[0:1, :], mod_ref[1:2, :], mod_ref[2:3, :]

    @pl.when(pl.program_id(1) == 0)
    def _():
        zbuf[0:8, :] = jnp.zeros((8, d), F32)

    rows = tm // n_sub
    for r0 in range(0, tm, rows):
        x = x_ref[r0:r0 + rows, :]
        h = _norm_modulate(_unit_rms(x), ng_ref[0:1, :], sc1, sh1).astype(BF16)
        us = []
        for c0 in range(0, d, V7X_MXU_COLS):
            cs = slice(c0, c0 + V7X_MXU_COLS)
            z = (_dot(h, win_ref[:, d + c0:d + c0 + V7X_MXU_COLS])
                 * _dot(h, win_ref[:, 2 * d + c0:2 * d + c0 + V7X_MXU_COLS]))
            zbuf[8 + r0:8 + r0 + rows, cs] = z
            conv = (ck_ref[0:1, cs] * zbuf[6 + r0:6 + r0 + rows, cs]
                    + ck_ref[1:2, cs] * zbuf[7 + r0:7 + r0 + rows, cs]
                    + ck_ref[2:3, cs] * z)
            us.append((_dot(h, win_ref[:, cs]) * conv).astype(BF16))
        y = _dot(jnp.concatenate(us, axis=1), wout_ref[...])
        o_ref[r0:r0 + rows, :] = x + _gated_norm(y, ng_ref[1:2, :], g1)
    zbuf[0:8, :] = zbuf[tm:tm + 8, :]


def _mixer(x, mod, norm_g, layer, w_in, conv_k, w_out, a_idx, casts):
    bsz, s, d = x.shape
    tm = WIDE_ROW_TILE
    row_spec = pl.BlockSpec((None, tm, d), lambda b, i: (b, i, 0))
    (out,), cast = _tiled_call(
        functools.partial(_mixer_kernel, n_sub=tm // SUB_ROWS), "conv_mixer", (bsz, s // tm),
        in_specs=[
            row_spec,
            pl.BlockSpec((None, None, 6, d), lambda b, i: (layer, b, 0, 0)),
            _resident((None, 4, d), lambda b, i: (layer, 0, 0)),
            _resident((None, d, 3 * d), lambda b, i: (a_idx, 0, 0)),
            _resident((None, 3, d), lambda b, i: (a_idx, 0, 0)),
            _resident((None, d, d), lambda b, i: (a_idx, 0, 0)),
        ],
        args=(x, mod, norm_g, w_in, conv_k, w_out),
        out_specs=[row_spec],
        out_shapes=[jax.ShapeDtypeStruct(x.shape, F32)],
        scratch=[pltpu.VMEM((tm + 8, d), F32)],
        casts=casts)
    return out, cast


def _ffn_kernel(x_ref, mod_ref, ng_ref, win_ref, wout_ref, o_ref, *, n_sub):
    sh2, sc2, g2 = mod_ref[3:4, :], mod_ref[4:5, :], mod_ref[5:6, :]
    f = wout_ref.shape[0]
    rows = x_ref.shape[0] // n_sub
    for r in range(n_sub):
        rs = slice(r * rows, (r + 1) * rows)
        x = x_ref[rs, :]
        h = _norm_modulate(_unit_rms(x), ng_ref[2:3, :], sc2, sh2).astype(BF16)
        g = _dot(h, win_ref[:, 0:f])
        u = _dot(h, win_ref[:, f:2 * f])
        y = _dot((jax.nn.silu(g) * u).astype(BF16), wout_ref[...])
        o_ref[rs, :] = x + _gated_norm(y, ng_ref[3:4, :], g2)


def _ffn(x, mod, norm_g, layer, w_in, w_out, casts):
    bsz, s, d = x.shape
    f = w_out.shape[0]
    tm = WIDE_ROW_TILE
    row_spec = pl.BlockSpec((None, tm, d), lambda b, i: (b, i, 0))
    (out,), cast = _tiled_call(
        functools.partial(_ffn_kernel, n_sub=WIDE_ROW_TILE // SUB_ROWS), "swiglu_ffn", (bsz, s // tm),
        in_specs=[
            row_spec,
            pl.BlockSpec((None, None, 6, d), lambda b, i: (layer, b, 0, 0)),
            _resident((None, 4, d), lambda b, i: (layer, 0, 0)),
            _resident((d, 2 * f), lambda b, i: (0, 0)),
            _resident((f, d), lambda b, i: (0, 0)),
        ],
        args=(x, mod, norm_g, w_in, w_out),
        out_specs=[row_spec],
        out_shapes=[jax.ShapeDtypeStruct(x.shape, F32)],
        casts=casts)
    return out, cast


def _proj_kernel(x_ref, mod_ref, kvmod_ref, ng_ref, kvg_ref, wkv_ref, wq_ref,
                 k_ref, v_ref, q_ref):
    d = x_ref.shape[1]
    xn = _unit_rms(x_ref[...])
    hkv = _norm_modulate(xn, kvg_ref[...], kvmod_ref[1:2, :], kvmod_ref[0:1, :]).astype(BF16)
    k_ref[...] = _dot(hkv, wkv_ref[:, 0:d]).astype(BF16)
    v_ref[...] = _dot(hkv, wkv_ref[:, d:2 * d]).astype(BF16)
    h = _norm_modulate(xn, ng_ref[0:1, :], mod_ref[1:2, :], mod_ref[0:1, :]).astype(BF16)
    q_ref[...] = (_dot(h, wq_ref[...]) * (LOG2_E * HEAD_DIM ** -0.5)).astype(BF16)


def _proj(x, mod, kvmod, norm_g, layer, kv_norm_g, w_kv, w_q, casts):
    bsz, s, d = x.shape
    tm = ROW_TILE
    row_spec = pl.BlockSpec((None, tm, d), lambda b, i: (b, i, 0))
    out = jax.ShapeDtypeStruct(x.shape, BF16)
    return _tiled_call(
        _proj_kernel, "qkv_proj", (bsz, s // tm),
        in_specs=[
            row_spec,
            pl.BlockSpec((None, None, 6, d), lambda b, i: (layer, b, 0, 0)),
            pl.BlockSpec((None, None, 2, d), lambda b, i: (0, b, 0, 0)),
            _resident((None, 4, d), lambda b, i: (layer, 0, 0)),
            _resident((1, d), lambda b, i: (0, 0)),
            _resident((d, 2 * d), lambda b, i: (0, 0)),
            _resident((d, d), lambda b, i: (0, 0)),
        ],
        args=(x, mod, kvmod, norm_g, kv_norm_g, w_kv, w_q),
        out_specs=[row_spec, row_spec, row_spec],
        out_shapes=[out, out, out],
        casts=casts)


def _bias_kernel(rb_ref, o_ref):
    n_heads = rb_ref.shape[0]
    n_tab = 2 * MAX_REL
    rb = rb_ref[:, 0:n_tab]
    hi = rb.astype(BF16)
    r1 = rb - hi.astype(F32)
    mid = r1.astype(BF16)
    lo = (r1 - mid.astype(F32)).astype(BF16)
    pieces = jnp.concatenate([hi, mid, lo], axis=0)
    rb_top = rb_ref[:, n_tab:n_tab + 1]
    u = lax.broadcasted_iota(jnp.int32, (1, WIN), 1)
    m = jnp.where(u >= WIN - CHUNK, u - WIN, u)
    idx = jnp.clip(N_LEFT_CHUNKS * CHUNK - m, -MAX_REL, MAX_REL) + MAX_REL
    row = lax.broadcasted_iota(jnp.int32, (n_tab, WIN), 0)
    onehot = jnp.where(row == idx, 1.0, 0.0).astype(BF16)
    t3 = _dot(pieces, onehot)
    w = t3[0:n_heads] + t3[n_heads:2 * n_heads] + t3[2 * n_heads:3 * n_heads]
    w = (w + jnp.where(idx == n_tab, rb_top, 0.0)) * LOG2_E
    col = lax.broadcasted_iota(jnp.int32, (CHUNK, WIN), 1)
    neg = jnp.finfo(F32).min
    for var in range(2):
        kk = col - var * CHUNK
        in_band = (kk >= 0) & (kk < BAND)
        for h in range(n_heads):
            t = pltpu.roll(jnp.broadcast_to(w[h:h + 1, :], (CHUNK, WIN)), var * CHUNK, 1,
                           stride=1, stride_axis=0)
            o_ref[var, h] = jnp.where(in_band, t, neg)[:, 0:SLAB]


def _bias_table(rel_bias, b_idx):
    n_heads, n_rel = rel_bias.shape[1:]
    return pl.pallas_call(
        _bias_kernel,
        grid=(1,),
        in_specs=[pl.BlockSpec((None, n_heads, n_rel), lambda i: (b_idx, 0, 0))],
        out_specs=pl.BlockSpec((2, n_heads, CHUNK, SLAB), lambda i: (0, 0, 0, 0)),
        out_shape=jax.ShapeDtypeStruct((2, n_heads, CHUNK, SLAB), F32),
        compiler_params=_params(1),
        name="rel_bias_table",
    )(rel_bias)


def _attn_tile(x_ref, q_ref, kwin, vwin, bias_ref, wo_ref, mod_ref, ng_ref, o_ref, obuf,
               *, first_tile):
    tm, d = q_ref.shape
    m_rows = 2 * GROUP * CHUNK
    low_head = lax.broadcasted_iota(jnp.int32, (CHUNK, V7X_LANES), 1) < HEAD_DIM
    neg = jnp.finfo(F32).min
    ones = jnp.ones((WIN, V7X_LANES), BF16)
    zcols = jnp.zeros((SOFTMAX_ROWS, WIN - SLAB), BF16)
    col = lax.broadcasted_iota(jnp.int32, (m_rows, WIN), 1) if first_tile else None

    def scores(g, p):
        w0 = g * GROUP * CHUNK
        ls = slice(p * V7X_LANES, (p + 1) * V7X_LANES)
        q2 = []
        for j in range(GROUP):
            qp = q_ref[w0 + j * CHUNK:w0 + (j + 1) * CHUNK, ls]
            zero = jnp.zeros_like(qp)
            q2 += [jnp.where(low_head, qp, zero), jnp.where(low_head, zero, qp)]
        s = lax.dot_general(jnp.concatenate(q2, axis=0), kwin[w0:w0 + WIN, ls],
                            (((1,), (1,)), ((), ())), preferred_element_type=F32)
        if first_tile and w0 < LEFT_ROWS:
            s = jnp.where(col < LEFT_ROWS - w0, neg, s)
        return s

    def attend(g, p, s):
        w0 = g * GROUP * CHUNK
        ls = slice(p * V7X_LANES, (p + 1) * V7X_LANES)
        blocks = []
        for j in range(GROUP):
            t0 = (j // 2) * V7X_LANES
            for b0 in range(0, 2 * CHUNK, SOFTMAX_ROWS):
                r0 = 2 * j * CHUNK + b0
                sl = (s[r0:r0 + SOFTMAX_ROWS, t0:t0 + SLAB]
                      + bias_ref[j % 2, p, b0:b0 + SOFTMAX_ROWS, :])
                ex = jnp.exp2(sl - jnp.max(sl, axis=-1, keepdims=True)).astype(BF16)
                blocks.append(jnp.concatenate([ex, zcols] if t0 == 0 else [zcols, ex], axis=1))
        v_ones = jnp.concatenate([vwin[w0:w0 + WIN, ls], ones], axis=1)
        o2 = _dot(jnp.concatenate(blocks, axis=0), v_ones)
        o2 = o2[:, 0:V7X_LANES] / o2[:, V7X_LANES:2 * V7X_LANES]
        for j in range(GROUP):
            lo = o2[(2 * j) * CHUNK:(2 * j + 1) * CHUNK, :]
            hi = o2[(2 * j + 1) * CHUNK:(2 * j + 2) * CHUNK, :]
            obuf[w0 + j * CHUNK:w0 + (j + 1) * CHUNK, ls] = jnp.where(low_head, lo, hi).astype(BF16)

    bodies = [(g, p) for g in range(tm // (GROUP * CHUNK)) for p in range(d // V7X_LANES)]
    pending = {n: scores(*bodies[n]) for n in range(SCORES_AHEAD)}
    for n, body in enumerate(bodies):
        if n + SCORES_AHEAD < len(bodies):
            pending[n + SCORES_AHEAD] = scores(*bodies[n + SCORES_AHEAD])
        attend(*body, pending.pop(n))
    y = _dot(obuf[...], wo_ref[...])
    o_ref[...] = x_ref[...] + _gated_norm(y, ng_ref[1:2, :], mod_ref[2:3, :])


def _attn_first_kernel(x_ref, q_ref, k_ref, v_ref, bias_ref, wo_ref, mod_ref, ng_ref, o_ref,
                       obuf, kwin, vwin):
    tm, d = q_ref.shape
    for src, win in ((k_ref, kwin), (v_ref, vwin)):
        win[0:LEFT_ROWS, :] = jnp.zeros((LEFT_ROWS, d), BF16)
        win[LEFT_ROWS:LEFT_ROWS + tm, :] = src[...]
    _attn_tile(x_ref, q_ref, kwin, vwin, bias_ref, wo_ref, mod_ref, ng_ref, o_ref, obuf,
               first_tile=True)


def _attn_rest_kernel(x_ref, q_ref, kwin, vwin, bias_ref, wo_ref, mod_ref, ng_ref, first_ref,
                      o_ref, obuf):
    @pl.when(pl.program_id(1) == 0)
    def _():
        o_ref[...] = first_ref[...]

    @pl.when(pl.program_id(1) > 0)
    def _():
        _attn_tile(x_ref, q_ref, kwin.at[0], vwin.at[0], bias_ref, wo_ref, mod_ref, ng_ref,
                   o_ref, obuf, first_tile=False)


def _attn(x, q, k, v, bias, w_o, mod, norm_g, layer):
    bsz, s, d = x.shape
    tm = WIDE_ROW_TILE
    n_tiles = s // tm
    assert tm % (GROUP * CHUNK) == 0 and tm % LEFT_ROWS == 0
    n_pairs = d // V7X_LANES
    scratch = [pltpu.VMEM((tm, d), BF16)]

    def call(first_tile, first_out=None):
        row_spec = pl.BlockSpec((None, tm, d), lambda b, i: (b, i, 0))
        if first_tile:
            kv_spec = row_spec
        else:
            kv_spec = pl.BlockSpec((pl.Element(1), pl.Element(LEFT_ROWS + tm), pl.Element(d)),
                                   lambda b, i: (b, jnp.maximum(i * (tm // LEFT_ROWS) - 1, 0) * LEFT_ROWS, 0))
        in_specs = [
            row_spec, row_spec, kv_spec, kv_spec,
            _resident((2, n_pairs, 2 * CHUNK, SLAB), lambda b, i: (0, 0, 0, 0)),
            _resident((d, d), lambda b, i: (0, 0)),
            pl.BlockSpec((None, None, 6, d), lambda b, i: (layer, b, 0, 0)),
            _resident((None, 4, d), lambda b, i: (layer, 0, 0)),
        ]
        args = [x, q, k, v, bias, w_o, mod, norm_g]
        if first_tile:
            kv_scratch = [pltpu.VMEM((LEFT_ROWS + tm, d), BF16)] * 2
        else:
            in_specs.append(pl.BlockSpec((None, tm, d), lambda b, i: (b, 0, 0)))
            args.append(first_out)
            kv_scratch = []
        return pl.pallas_call(
            _attn_first_kernel if first_tile else _attn_rest_kernel,
            grid=(bsz, 1 if first_tile else n_tiles),
            in_specs=in_specs,
            out_specs=row_spec,
            out_shape=jax.ShapeDtypeStruct((bsz, tm if first_tile else s, d), F32),
            scratch_shapes=scratch + kv_scratch,
            compiler_params=_params(2),
            name="band_attention_first" if first_tile else "band_attention",
        )(*args)

    return call(False, call(True))


def kernel(x, c, mod_w, mod_b, norm_g, ffn_w_in, ffn_w_out, conv_w_in, conv_k, conv_w_out,
           kv_mod_w, kv_mod_b, kv_norm_g, w_kv, attn_w_q, attn_w_o, rel_bias):
    bsz, s, d = x.shape
    depth = mod_w.shape[0]
    n_a = conv_w_in.shape[0]
    n_heads = rel_bias.shape[1]
    assert d // n_heads == HEAD_DIM and s % WIDE_ROW_TILE == 0 and WIDE_ROW_TILE % ROW_TILE == 0

    mod = _modvec(c, mod_w, mod_b.reshape(depth, 1, 6 * d)).reshape(depth, bsz, 6, d)
    kvmod = _modvec(c, kv_mod_w[None], kv_mod_b.reshape(1, 1, 2 * d)).reshape(1, bsz, 2, d)

    conv_w_in, conv_w_out = conv_w_in.astype(BF16), conv_w_out.astype(BF16)
    kv_norm_g = kv_norm_g.reshape(1, d)
    w_kv = w_kv[None]

    k = v = None
    attn_w = None
    ffn_w = None
    for layer in range(depth):
        ffn_casts = [] if ffn_w else [(ffn_w_in, layer), (ffn_w_out, layer)]
        if layer < n_a:
            x, cast = _mixer(x, mod, norm_g, layer, conv_w_in, conv_k, conv_w_out, layer, ffn_casts)
        else:
            bi = layer - n_a
            if attn_w is None:
                attn_w = (w_kv[0].astype(BF16), attn_w_q[bi].astype(BF16), attn_w_o[bi].astype(BF16))
            w_kv_b, w_q_b, w_o_b = attn_w
            (k_new, v_new, q), cast = _proj(x, mod, kvmod, norm_g, layer, kv_norm_g,
                                            w_kv_b, w_q_b, ffn_casts)
            if layer == n_a:
                k, v = k_new, v_new
            bias = _bias_table(rel_bias, bi).reshape(2, n_heads // 2, 2 * CHUNK, SLAB)
            x = _attn(x, q, k, v, bias, w_o_b, mod, norm_g, layer)
        w_in, w_out = ffn_w or cast
        nxt = layer + 1
        next_casts = []
        if nxt < depth:
            next_casts = [(ffn_w_in, nxt), (ffn_w_out, nxt)]
            if nxt >= n_a:
                next_casts += [(w_kv, 0), (attn_w_q, nxt - n_a), (attn_w_o, nxt - n_a)]
        x, cast = _ffn(x, mod, norm_g, layer, w_in, w_out, next_casts)
        ffn_w = tuple(cast[0:2]) if nxt < depth else None
        attn_w = tuple(cast[2:5]) if len(cast) == 5 else None
    return x
```

```python
import functools
import math

import jax
import jax.numpy as jnp
from jax import lax
from jax.experimental import pallas as pl
from jax.experimental.pallas import tpu as pltpu

EPS = 1e-6
CHUNK = 64
N_LEFT_CHUNKS = 8
LEFT_ROWS = N_LEFT_CHUNKS * CHUNK
BAND = (N_LEFT_CHUNKS + 1) * CHUNK
SLAB = BAND + CHUNK
GROUP = 4
SCORES_AHEAD = 1
SOFTMAX_ROWS = 32
WIN = (GROUP + N_LEFT_CHUNKS) * CHUNK
HEAD_DIM = 64
MAX_REL = 2 * CHUNK
LOG2_E = math.log2(math.e)

V7X_LANES = 128
V7X_BF16_SUBLANES = 16
V7X_MXU_COLS = 256
ROW_TILE = 512
WIDE_ROW_TILE = 1024
SUB_ROWS = 256
V7X_VMEM_LIMIT = 56 * 1024 * 1024

BF16 = jnp.bfloat16
F32 = jnp.float32


def _dot(a, b):
    return jnp.dot(a, b, preferred_element_type=F32)


def _unit_rms(x):
    return x * lax.rsqrt(jnp.mean(x * x, axis=-1, keepdims=True) + EPS)


def _norm_modulate(xn, g, scale, shift):
    return xn * (g * (1.0 + scale)) + shift


def _gated_norm(y, g, gate):
    return _unit_rms(y) * (gate * g)


def _resident(block_shape, index_map):
    return pl.BlockSpec(block_shape, index_map, pipeline_mode=pl.Buffered(1))


def _params(n_grid_dims):
    return pltpu.CompilerParams(
        dimension_semantics=("arbitrary",) * n_grid_dims,
        vmem_limit_bytes=V7X_VMEM_LIMIT)


def _with_casts(body, n_in, n_out, n_cast):
    def wrapped(*refs):
        srcs = refs[n_in:n_in + n_cast]
        outs_end = n_in + n_cast + n_out
        dsts = refs[outs_end:outs_end + n_cast]
        for src, dst in zip(srcs, dsts):
            dst[...] = src[...].astype(BF16)
        body(*refs[:n_in], *refs[n_in + n_cast:outs_end], *refs[outs_end + n_cast:])
    return wrapped


def _tiled_call(body, name, grid, in_specs, args, out_specs, out_shapes, scratch=(), casts=()):
    bsz, n_tiles = grid
    c_in, c_out, c_shapes, c_args = [], [], [], []
    for w, idx in casts:
        r, c = w.shape[1:]
        rows, cols = r // n_tiles, c // bsz
        assert rows * n_tiles == r and rows % V7X_BF16_SUBLANES == 0
        assert cols * bsz == c and cols % V7X_LANES == 0
        c_in.append(pl.BlockSpec((None, rows, cols), lambda b, i, idx=idx: (idx, i, b)))
        c_out.append(pl.BlockSpec((rows, cols), lambda b, i: (i, b)))
        c_shapes.append(jax.ShapeDtypeStruct((r, c), BF16))
        c_args.append(w)
    res = pl.pallas_call(
        _with_casts(body, len(in_specs), len(out_specs), len(casts)),
        grid=grid,
        in_specs=list(in_specs) + c_in,
        out_specs=list(out_specs) + c_out,
        out_shape=list(out_shapes) + c_shapes,
        scratch_shapes=list(scratch),
        compiler_params=_params(len(grid)),
        name=name,
    )(*args, *c_args)
    return res[:len(out_specs)], res[len(out_specs):]


def _modvec_kernel(c_ref, w_ref, b_ref, o_ref):
    s = jax.nn.silu(c_ref[...]).astype(BF16)
    o_ref[...] = _dot(s, w_ref[...].astype(BF16)) + b_ref[...]


def _modvec(c, w, b, col_tile=1024):
    n_layers, d, n = w.shape
    bsz = c.shape[0]
    return pl.pallas_call(
        _modvec_kernel,
        grid=(n_layers, n // col_tile),
        in_specs=[
            pl.BlockSpec((bsz, d), lambda l, j: (0, 0)),
            pl.BlockSpec((None, d, col_tile), lambda l, j: (l, 0, j)),
            pl.BlockSpec((None, 1, col_tile), lambda l, j: (l, 0, j)),
        ],
        out_specs=pl.BlockSpec((None, bsz, col_tile), lambda l, j: (l, 0, j)),
        out_shape=jax.ShapeDtypeStruct((n_layers, bsz, n), F32),
        compiler_params=_params(2),
        name="modvec",
    )(c, w, b)


def _mixer_kernel(x_ref, mod_ref, ng_ref, win_ref, ck_ref, wout_ref, o_ref, zbuf, *, n_sub):
    tm, d = x_ref.shape
    sh1, sc1, g1 = mod_ref[0:1, :], mod_ref[1:2, :], mod_ref[2:3, :]

    @pl.when(pl.program_id(1) == 0)
    def _():
        zbuf[0:8, :] = jnp.zeros((8, d), F32)

    rows = tm // n_sub
    for r0 in range(0, tm, rows):
        x = x_ref[r0:r0 + rows, :]
        h = _norm_modulate(_unit_rms(x), ng_ref[0:1, :], sc1, sh1).astype(BF16)
        us = []
        for c0 in range(0, d, V7X_MXU_COLS):
            cs = slice(c0, c0 + V7X_MXU_COLS)
            z = (_dot(h, win_ref[:, d + c0:d + c0 + V7X_MXU_COLS])
                 * _dot(h, win_ref[:, 2 * d + c0:2 * d + c0 + V7X_MXU_COLS]))
            zbuf[8 + r0:8 + r0 + rows, cs] = z
            conv = (ck_ref[0:1, cs] * zbuf[6 + r0:6 + r0 + rows, cs]
                    + ck_ref[1:2, cs] * zbuf[7 + r0:7 + r0 + rows, cs]
                    + ck_ref[2:3, cs] * z)
            us.append((_dot(h, win_ref[:, cs]) * conv).astype(BF16))
        y = _dot(jnp.concatenate(us, axis=1), wout_ref[...])
        o_ref[r0:r0 + rows, :] = x + _gated_norm(y, ng_ref[1:2, :], g1)
    zbuf[0:8, :] = zbuf[tm:tm + 8, :]


def _mixer(x, mod, norm_g, layer, w_in, conv_k, w_out, a_idx, casts):
    bsz, s, d = x.shape
    tm = WIDE_ROW_TILE
    row_spec = pl.BlockSpec((None, tm, d), lambda b, i: (b, i, 0))
    (out,), cast = _tiled_call(
        functools.partial(_mixer_kernel, n_sub=tm // SUB_ROWS), "conv_mixer", (bsz, s // tm),
        in_specs=[
            row_spec,
            pl.BlockSpec((None, None, 6, d), lambda b, i: (layer, b, 0, 0)),
            _resident((None, 4, d), lambda b, i: (layer, 0, 0)),
            _resident((None, d, 3 * d), lambda b, i: (a_idx, 0, 0)),
            _resident((None, 3, d), lambda b, i: (a_idx, 0, 0)),
            _resident((None, d, d), lambda b, i: (a_idx, 0, 0)),
        ],
        args=(x, mod, norm_g, w_in, conv_k, w_out),
        out_specs=[row_spec],
        out_shapes=[jax.ShapeDtypeStruct(x.shape, F32)],
        scratch=[pltpu.VMEM((tm + 8, d), F32)],
        casts=casts)
    return out, cast


def _ffn_kernel(x_ref, mod_ref, ng_ref, win_ref, wout_ref, o_ref, *, n_sub):
    sh2, sc2, g2 = mod_ref[3:4, :], mod_ref[4:5, :], mod_ref[5:6, :]
    f = wout_ref.shape[0]
    rows = x_ref.shape[0] // n_sub
    for r in range(n_sub):
        rs = slice(r * rows, (r + 1) * rows)
        x = x_ref[rs, :]
        h = _norm_modulate(_unit_rms(x), ng_ref[2:3, :], sc2, sh2).astype(BF16)
        g = _dot(h, win_ref[:, 0:f])
        u = _dot(h, win_ref[:, f:2 * f])
        y = _dot((jax.nn.silu(g) * u).astype(BF16), wout_ref[...])
        o_ref[rs, :] = x + _gated_norm(y, ng_ref[3:4, :], g2)


def _ffn(x, mod, norm_g, layer, w_in, w_out, casts):
    bsz, s, d = x.shape
    f = w_out.shape[0]
    tm = WIDE_ROW_TILE
    row_spec = pl.BlockSpec((None, tm, d), lambda b, i: (b, i, 0))
    (out,), cast = _tiled_call(
        functools.partial(_ffn_kernel, n_sub=WIDE_ROW_TILE // SUB_ROWS), "swiglu_ffn", (bsz, s // tm),
        in_specs=[
            row_spec,
            pl.BlockSpec((None, None, 6, d), lambda b, i: (layer, b, 0, 0)),
            _resident((None, 4, d), lambda b, i: (layer, 0, 0)),
            _resident((d, 2 * f), lambda b, i: (0, 0)),
            _resident((f, d), lambda b, i: (0, 0)),
        ],
        args=(x, mod, norm_g, w_in, w_out),
        out_specs=[row_spec],
        out_shapes=[jax.ShapeDtypeStruct(x.shape, F32)],
        casts=casts)
    return out, cast


def _proj_kernel(x_ref, mod_ref, kvmod_ref, ng_ref, kvg_ref, wkv_ref, wq_ref,
                 k_ref, v_ref, q_ref):
    d = x_ref.shape[1]
    xn = _unit_rms(x_ref[...])
    hkv = _norm_modulate(xn, kvg_ref[...], kvmod_ref[1:2, :], kvmod_ref[0:1, :]).astype(BF16)
    k_ref[...] = _dot(hkv, wkv_ref[:, 0:d]).astype(BF16)
    v_ref[...] = _dot(hkv, wkv_ref[:, d:2 * d]).astype(BF16)
    h = _norm_modulate(xn, ng_ref[0:1, :], mod_ref[1:2, :], mod_ref[0:1, :]).astype(BF16)
    q_ref[...] = (_dot(h, wq_ref[...]) * (LOG2_E * HEAD_DIM ** -0.5)).astype(BF16)


def _proj(x, mod, kvmod, norm_g, layer, kv_norm_g, w_kv, w_q, casts):
    bsz, s, d = x.shape
    tm = ROW_TILE
    row_spec = pl.BlockSpec((None, tm, d), lambda b, i: (b, i, 0))
    out = jax.ShapeDtypeStruct(x.shape, BF16)
    return _tiled_call(
        _proj_kernel, "qkv_proj", (bsz, s // tm),
        in_specs=[
            row_spec,
            pl.BlockSpec((None, None, 6, d), lambda b, i: (layer, b, 0, 0)),
            pl.BlockSpec((None, None, 2, d), lambda b, i: (0, b, 0, 0)),
            _resident((None, 4, d), lambda b, i: (layer, 0, 0)),
            _resident((1, d), lambda b, i: (0, 0)),
            _resident((d, 2 * d), lambda b, i: (0, 0)),
            _resident((d, d), lambda b, i: (0, 0)),
        ],
        args=(x, mod, kvmod, norm_g, kv_norm_g, w_kv, w_q),
        out_specs=[row_spec, row_spec, row_spec],
        out_shapes=[out, out, out],
        casts=casts)


def _bias_kernel(rb_ref, o_ref):
    n_heads = rb_ref.shape[0]
    n_tab = 2 * MAX_REL
    rb = rb_ref[:, 0:n_tab]
    hi = rb.astype(BF16)
    r1 = rb - hi.astype(F32)
    mid = r1.astype(BF16)
    lo = (r1 - mid.astype(F32)).astype(BF16)
    pieces = jnp.concatenate([hi, mid, lo], axis=0)
    rb_top = rb_ref[:, n_tab:n_tab + 1]
    u = lax.broadcasted_iota(jnp.int32, (1, WIN), 1)
    m = jnp.where(u >= WIN - CHUNK, u - WIN, u)
    idx = jnp.clip(N_LEFT_CHUNKS * CHUNK - m, -MAX_REL, MAX_REL) + MAX_REL
    row = lax.broadcasted_iota(jnp.int32, (n_tab, WIN), 0)
    onehot = jnp.where(row == idx, 1.0, 0.0).astype(BF16)
    t3 = _dot(pieces, onehot)
    w = t3[0:n_heads] + t3[n_heads:2 * n_heads] + t3[2 * n_heads:3 * n_heads]
    w = (w + jnp.where(idx == n_tab, rb_top, 0.0)) * LOG2_E
    col = lax.broadcasted_iota(jnp.int32, (CHUNK, WIN), 1)
    neg = jnp.finfo(F32).min
    for var in range(2):
        kk = col - var * CHUNK
        in_band = (kk >= 0) & (kk < BAND)
        for h in range(n_heads):
            t = pltpu.roll(jnp.broadcast_to(w[h:h + 1, :], (CHUNK, WIN)), var * CHUNK, 1,
                           stride=1, stride_axis=0)
            o_ref[var, h] = jnp.where(in_band, t, neg)[:, 0:SLAB]


def _bias_table(rel_bias, b_idx):
    n_heads, n_rel = rel_bias.shape[1:]
    return pl.pallas_call(
        _bias_kernel,
        grid=(1,),
        in_specs=[pl.BlockSpec((None, n_heads, n_rel), lambda i: (b_idx, 0, 0))],
        out_specs=pl.BlockSpec((2, n_heads, CHUNK, SLAB), lambda i: (0, 0, 0, 0)),
        out_shape=jax.ShapeDtypeStruct((2, n_heads, CHUNK, SLAB), F32),
        compiler_params=_params(1),
        name="rel_bias_table",
    )(rel_bias)


def _attn_tile(x_ref, q_ref, kwin, vwin, bias_ref, wo_ref, mod_ref, ng_ref, o_ref, obuf,
               *, first_tile, row0=0, key_shift=0):
    tm, d = q_ref.shape
    m_rows = 2 * GROUP * CHUNK
    low_head = lax.broadcasted_iota(jnp.int32, (CHUNK, V7X_LANES), 1) < HEAD_DIM
    neg = jnp.finfo(F32).min
    ones = jnp.ones((WIN, V7X_LANES), BF16)
    zcols = jnp.zeros((SOFTMAX_ROWS, WIN - SLAB), BF16)
    col = lax.broadcasted_iota(jnp.int32, (m_rows, WIN), 1) if first_tile else None

    def scores(g, p):
        q0 = g * GROUP * CHUNK
        w0 = q0 + key_shift
        ls = slice(p * V7X_LANES, (p + 1) * V7X_LANES)
        q2 = []
        for j in range(GROUP):
            qp = q_ref[q0 + j * CHUNK:q0 + (j + 1) * CHUNK, ls]
            zero = jnp.zeros_like(qp)
            q2 += [jnp.where(low_head, qp, zero), jnp.where(low_head, zero, qp)]
        s = lax.dot_general(jnp.concatenate(q2, axis=0), kwin[w0:w0 + WIN, ls],
                            (((1,), (1,)), ((), ())), preferred_element_type=F32)
        if first_tile and w0 < LEFT_ROWS:
            s = jnp.where(col < LEFT_ROWS - w0, neg, s)
        return s

    def attend(g, p, s):
        q0 = g * GROUP * CHUNK
        w0 = q0 + key_shift
        ls = slice(p * V7X_LANES, (p + 1) * V7X_LANES)
        blocks = []
        for j in range(GROUP):
            t0 = (j // 2) * V7X_LANES
            for b0 in range(0, 2 * CHUNK, SOFTMAX_ROWS):
                r0 = 2 * j * CHUNK + b0
                sl = (s[r0:r0 + SOFTMAX_ROWS, t0:t0 + SLAB]
                      + bias_ref[j % 2, p, b0:b0 + SOFTMAX_ROWS, :])
                ex = jnp.exp2(sl - jnp.max(sl, axis=-1, keepdims=True)).astype(BF16)
                blocks.append(jnp.concatenate([ex, zcols] if t0 == 0 else [zcols, ex], axis=1))
        v_ones = jnp.concatenate([vwin[w0:w0 + WIN, ls], ones], axis=1)
        o2 = _dot(jnp.concatenate(blocks, axis=0), v_ones)
        o2 = o2[:, 0:V7X_LANES] / o2[:, V7X_LANES:2 * V7X_LANES]
        for j in range(GROUP):
            lo = o2[(2 * j) * CHUNK:(2 * j + 1) * CHUNK, :]
            hi = o2[(2 * j + 1) * CHUNK:(2 * j + 2) * CHUNK, :]
            obuf[q0 + j * CHUNK:q0 + (j + 1) * CHUNK, ls] = jnp.where(low_head, lo, hi).astype(BF16)

    bodies = [(g, p) for g in range(row0 // (GROUP * CHUNK), tm // (GROUP * CHUNK))
              for p in range(d // V7X_LANES)]
    pending = {n: scores(*bodies[n]) for n in range(SCORES_AHEAD)}
    for n, body in enumerate(bodies):
        if n + SCORES_AHEAD < len(bodies):
            pending[n + SCORES_AHEAD] = scores(*bodies[n + SCORES_AHEAD])
        attend(*body, pending.pop(n))
    y = _dot(obuf[row0:tm, :], wo_ref[...])
    o_ref[row0:tm, :] = x_ref[row0:tm, :] + _gated_norm(y, ng_ref[1:2, :], mod_ref[2:3, :])


def _attn_first_kernel(x_ref, q_ref, k_ref, v_ref, bias_ref, wo_ref, mod_ref, ng_ref, o_ref,
                       obuf, kwin, vwin):
    tm, d = q_ref.shape
    for src, win in ((k_ref, kwin), (v_ref, vwin)):
        win[0:LEFT_ROWS, :] = jnp.zeros((LEFT_ROWS, d), BF16)
        win[LEFT_ROWS:LEFT_ROWS + tm, :] = src[...]
    _attn_tile(x_ref, q_ref, kwin, vwin, bias_ref, wo_ref, mod_ref, ng_ref, o_ref, obuf,
               first_tile=True)


def _attn_rest_kernel(x_ref, q_ref, kwin, vwin, bias_ref, wo_ref, mod_ref, ng_ref, first_ref,
                      o_ref, obuf):
    @pl.when(pl.program_id(1) == 0)
    def _():
        o_ref[0:LEFT_ROWS, :] = first_ref[...]
        _attn_tile(x_ref, q_ref, kwin.at[0], vwin.at[0], bias_ref, wo_ref, mod_ref, ng_ref,
                   o_ref, obuf, first_tile=False, row0=LEFT_ROWS, key_shift=-LEFT_ROWS)

    @pl.when(pl.program_id(1) > 0)
    def _():
        _attn_tile(x_ref, q_ref, kwin.at[0], vwin.at[0], bias_ref, wo_ref, mod_ref, ng_ref,
                   o_ref, obuf, first_tile=False)


def _attn(x, q, k, v, bias, w_o, mod, norm_g, layer):
    bsz, s, d = x.shape
    tm = WIDE_ROW_TILE
    n_tiles = s // tm
    assert tm % (GROUP * CHUNK) == 0 and tm % LEFT_ROWS == 0
    n_pairs = d // V7X_LANES

    def call(first_tile, first_out=None):
        rows = LEFT_ROWS if first_tile else tm
        row_spec = pl.BlockSpec((None, rows, d), lambda b, i: (b, i, 0))
        if first_tile:
            kv_spec = row_spec
        else:
            kv_spec = pl.BlockSpec((pl.Element(1), pl.Element(LEFT_ROWS + tm), pl.Element(d)),
                                   lambda b, i: (b, jnp.maximum(i * (tm // LEFT_ROWS) - 1, 0) * LEFT_ROWS, 0))
        in_specs = [
            row_spec, row_spec, kv_spec, kv_spec,
            _resident((2, n_pairs, 2 * CHUNK, SLAB), lambda b, i: (0, 0, 0, 0)),
            _resident((d, d), lambda b, i: (0, 0)),
            pl.BlockSpec((None, None, 6, d), lambda b, i: (layer, b, 0, 0)),
            _resident((None, 4, d), lambda b, i: (layer, 0, 0)),
        ]
        args = [x, q, k, v, bias, w_o, mod, norm_g]
        if first_tile:
            kv_scratch = [pltpu.VMEM((2 * LEFT_ROWS, d), BF16)] * 2
        else:
            in_specs.append(pl.BlockSpec((None, LEFT_ROWS, d), lambda b, i: (b, 0, 0)))
            args.append(first_out)
            kv_scratch = []
        return pl.pallas_call(
            _attn_first_kernel if first_tile else _attn_rest_kernel,
            grid=(bsz, 1 if first_tile else n_tiles),
            in_specs=in_specs,
            out_specs=row_spec,
            out_shape=jax.ShapeDtypeStruct((bsz, rows if first_tile else s, d), F32),
            scratch_shapes=[pltpu.VMEM((rows, d), BF16)] + kv_scratch,
            compiler_params=_params(2),
            name="band_attention_first" if first_tile else "band_attention",
        )(*args)

    return call(False, call(True))


def kernel(x, c, mod_w, mod_b, norm_g, ffn_w_in, ffn_w_out, conv_w_in, conv_k, conv_w_out,
           kv_mod_w, kv_mod_b, kv_norm_g, w_kv, attn_w_q, attn_w_o, rel_bias):
    bsz, s, d = x.shape
    depth = mod_w.shape[0]
    n_a = conv_w_in.shape[0]
    n_heads = rel_bias.shape[1]
    assert d // n_heads == HEAD_DIM and s % WIDE_ROW_TILE == 0 and WIDE_ROW_TILE % ROW_TILE == 0

    mod = _modvec(c, mod_w, mod_b.reshape(depth, 1, 6 * d)).reshape(depth, bsz, 6, d)
    kvmod = _modvec(c, kv_mod_w[None], kv_mod_b.reshape(1, 1, 2 * d)).reshape(1, bsz, 2, d)

    conv_w_in, conv_w_out = conv_w_in.astype(BF16), conv_w_out.astype(BF16)
    kv_norm_g = kv_norm_g.reshape(1, d)
    w_kv = w_kv[None]

    k = v = None
    attn_w = None
    ffn_w = None
    for layer in range(depth):
        ffn_casts = [] if ffn_w else [(ffn_w_in, layer), (ffn_w_out, layer)]
        if layer < n_a:
            x, cast = _mixer(x, mod, norm_g, layer, conv_w_in, conv_k, conv_w_out, layer, ffn_casts)
        else:
            bi = layer - n_a
            if attn_w is None:
                attn_w = (w_kv[0].astype(BF16), attn_w_q[bi].astype(BF16), attn_w_o[bi].astype(BF16))
            w_kv_b, w_q_b, w_o_b = attn_w
            (k_new, v_new, q), cast = _proj(x, mod, kvmod, norm_g, layer, kv_norm_g,
                                            w_kv_b, w_q_b, ffn_casts)
            if layer == n_a:
                k, v = k_new, v_new
            bias = _bias_table(rel_bias, bi).reshape(2, n_heads // 2, 2 * CHUNK, SLAB)
            x = _attn(x, q, k, v, bias, w_o_b, mod, norm_g, layer)
        w_in, w_out = ffn_w or cast
        nxt = layer + 1
        next_casts = []
        if nxt < depth:
            next_casts = [(ffn_w_in, nxt), (ffn_w_out, nxt)]
            if nxt >= n_a:
                next_casts += [(w_kv, 0), (attn_w_q, nxt - n_a), (attn_w_o, nxt - n_a)]
        x, cast = _ffn(x, mod, norm_g, layer, w_in, w_out, next_casts)
        ffn_w = tuple(cast[0:2]) if nxt < depth else None
        attn_w = tuple(cast[2:5]) if len(cast) == 5 else None
    return x
```

```python
import functools
import math

import jax
import jax.numpy as jnp
from jax import lax
from jax.experimental import pallas as pl
from jax.experimental.pallas import tpu as pltpu

EPS = 1e-6
CHUNK = 64
N_LEFT_CHUNKS = 8
LEFT_ROWS = N_LEFT_CHUNKS * CHUNK
BAND = (N_LEFT_CHUNKS + 1) * CHUNK
SLAB = BAND + CHUNK
GROUP = 4
SCORES_AHEAD = 1
SOFTMAX_ROWS = 32
WIN = (GROUP + N_LEFT_CHUNKS) * CHUNK
HEAD_DIM = 64
MAX_REL = 2 * CHUNK
LOG2_E = math.log2(math.e)

V7X_LANES = 128
V7X_BF16_SUBLANES = 16
V7X_MXU_COLS = 256
ROW_TILE = 512
WIDE_ROW_TILE = 1024
SUB_ROWS = 256
V7X_VMEM_LIMIT = 56 * 1024 * 1024

BF16 = jnp.bfloat16
F32 = jnp.float32


def _dot(a, b):
    return jnp.dot(a, b, preferred_element_type=F32)


def _unit_rms(x):
    return x * lax.rsqrt(jnp.mean(x * x, axis=-1, keepdims=True) + EPS)


def _norm_modulate(xn, g, scale, shift):
    return xn * (g * (1.0 + scale)) + shift


def _gated_norm(y, g, gate):
    return _unit_rms(y) * (gate * g)


def _resident(block_shape, index_map):
    return pl.BlockSpec(block_shape, index_map, pipeline_mode=pl.Buffered(1))


def _params(n_grid_dims):
    return pltpu.CompilerParams(
        dimension_semantics=("arbitrary",) * n_grid_dims,
        vmem_limit_bytes=V7X_VMEM_LIMIT)


def _with_casts(body, n_in, n_out, n_cast):
    def wrapped(*refs):
        srcs = refs[n_in:n_in + n_cast]
        outs_end = n_in + n_cast + n_out
        dsts = refs[outs_end:outs_end + n_cast]
        for src, dst in zip(srcs, dsts):
            dst[...] = src[...].astype(BF16)
        body(*refs[:n_in], *refs[n_in + n_cast:outs_end], *refs[outs_end + n_cast:])
    return wrapped


def _tiled_call(body, name, grid, in_specs, args, out_specs, out_shapes, scratch=(), casts=()):
    bsz, n_tiles = grid
    c_in, c_out, c_shapes, c_args = [], [], [], []
    for w, idx in casts:
        r, c = w.shape[1:]
        rows, cols = r // n_tiles, c // bsz
        assert rows * n_tiles == r and rows % V7X_BF16_SUBLANES == 0
        assert cols * bsz == c and cols % V7X_LANES == 0
        c_in.append(pl.BlockSpec((None, rows, cols), lambda b, i, idx=idx: (idx, i, b)))
        c_out.append(pl.BlockSpec((rows, cols), lambda b, i: (i, b)))
        c_shapes.append(jax.ShapeDtypeStruct((r, c), BF16))
        c_args.append(w)
    res = pl.pallas_call(
        _with_casts(body, len(in_specs), len(out_specs), len(casts)),
        grid=grid,
        in_specs=list(in_specs) + c_in,
        out_specs=list(out_specs) + c_out,
        out_shape=list(out_shapes) + c_shapes,
        scratch_shapes=list(scratch),
        compiler_params=_params(len(grid)),
        name=name,
    )(*args, *c_args)
    return res[:len(out_specs)], res[len(out_specs):]


def _modvec_kernel(c_ref, w_ref, b_ref, o_ref):
    s = jax.nn.silu(c_ref[...]).astype(BF16)
    o_ref[...] = _dot(s, w_ref[...].astype(BF16)) + b_ref[...]


def _modvec(c, w, b, col_tile=1024):
    n_layers, d, n = w.shape
    bsz = c.shape[0]
    return pl.pallas_call(
        _modvec_kernel,
        grid=(n_layers, n // col_tile),
        in_specs=[
            pl.BlockSpec((bsz, d), lambda l, j: (0, 0)),
            pl.BlockSpec((None, d, col_tile), lambda l, j: (l, 0, j)),
            pl.BlockSpec((None, 1, col_tile), lambda l, j: (l, 0, j)),
        ],
        out_specs=pl.BlockSpec((None, bsz, col_tile), lambda l, j: (l, 0, j)),
        out_shape=jax.ShapeDtypeStruct((n_layers, bsz, n), F32),
        compiler_params=_params(2),
        name="modvec",
    )(c, w, b)


def _mixer_kernel(x_ref, mod_ref, ng_ref, win_ref, ck_ref, wout_ref, o_ref, zbuf, *, n_sub):
    tm, d = x_ref.shape
    sh1, sc1, g1 = mod_ref[0:1, :], mod_ref[1:2, :], mod_ref[2:3, :]

    @pl.when(pl.program_id(1) == 0)
    def _():
        zbuf[0:8, :] = jnp.zeros((8, d), F32)

    rows = tm // n_sub
    for r0 in range(0, tm, rows):
        x = x_ref[r0:r0 + rows, :]
        h = _norm_modulate(_unit_rms(x), ng_ref[0:1, :], sc1, sh1).astype(BF16)
        us = []
        for c0 in range(0, d, V7X_MXU_COLS):
            cs = slice(c0, c0 + V7X_MXU_COLS)
            z = (_dot(h, win_ref[:, d + c0:d + c0 + V7X_MXU_COLS])
                 * _dot(h, win_ref[:, 2 * d + c0:2 * d + c0 + V7X_MXU_COLS]))
            zbuf[8 + r0:8 + r0 + rows, cs] = z
            conv = (ck_ref[0:1, cs] * zbuf[6 + r0:6 + r0 + rows, cs]
                    + ck_ref[1:2, cs] * zbuf[7 + r0:7 + r0 + rows, cs]
                    + ck_ref[2:3, cs] * z)
            us.append((_dot(h, win_ref[:, cs]) * conv).astype(BF16))
        y = _dot(jnp.concatenate(us, axis=1), wout_ref[...])
        o_ref[r0:r0 + rows, :] = x + _gated_norm(y, ng_ref[1:2, :], g1)
    zbuf[0:8, :] = zbuf[tm:tm + 8, :]


def _mixer(x, mod, norm_g, layer, w_in, conv_k, w_out, a_idx, casts):
    bsz, s, d = x.shape
    tm = WIDE_ROW_TILE
    row_spec = pl.BlockSpec((None, tm, d), lambda b, i: (b, i, 0))
    (out,), cast = _tiled_call(
        functools.partial(_mixer_kernel, n_sub=tm // SUB_ROWS), "conv_mixer", (bsz, s // tm),
        in_specs=[
            row_spec,
            pl.BlockSpec((None, None, 6, d), lambda b, i: (layer, b, 0, 0)),
            _resident((None, 4, d), lambda b, i: (layer, 0, 0)),
            _resident((None, d, 3 * d), lambda b, i: (a_idx, 0, 0)),
            _resident((None, 3, d), lambda b, i: (a_idx, 0, 0)),
            _resident((None, d, d), lambda b, i: (a_idx, 0, 0)),
        ],
        args=(x, mod, norm_g, w_in, conv_k, w_out),
        out_specs=[row_spec],
        out_shapes=[jax.ShapeDtypeStruct(x.shape, F32)],
        scratch=[pltpu.VMEM((tm + 8, d), F32)],
        casts=casts)
    return out, cast


def _ffn_kernel(x_ref, mod_ref, ng_ref, win_ref, wout_ref, o_ref, *, n_sub):
    sh2, sc2, g2 = mod_ref[3:4, :], mod_ref[4:5, :], mod_ref[5:6, :]
    f = wout_ref.shape[0]
    rows = x_ref.shape[0] // n_sub
    for r in range(n_sub):
        rs = slice(r * rows, (r + 1) * rows)
        x = x_ref[rs, :]
        h = _norm_modulate(_unit_rms(x), ng_ref[2:3, :], sc2, sh2).astype(BF16)
        g = _dot(h, win_ref[:, 0:f])
        u = _dot(h, win_ref[:, f:2 * f])
        y = _dot((jax.nn.silu(g) * u).astype(BF16), wout_ref[...])
        o_ref[rs, :] = x + _gated_norm(y, ng_ref[3:4, :], g2)


def _ffn(x, mod, norm_g, layer, w_in, w_out, casts):
    bsz, s, d = x.shape
    f = w_out.shape[0]
    tm = WIDE_ROW_TILE
    row_spec = pl.BlockSpec((None, tm, d), lambda b, i: (b, i, 0))
    (out,), cast = _tiled_call(
        functools.partial(_ffn_kernel, n_sub=WIDE_ROW_TILE // SUB_ROWS), "swiglu_ffn", (bsz, s // tm),
        in_specs=[
            row_spec,
            pl.BlockSpec((None, None, 6, d), lambda b, i: (layer, b, 0, 0)),
            _resident((None, 4, d), lambda b, i: (layer, 0, 0)),
            _resident((d, 2 * f), lambda b, i: (0, 0)),
            _resident((f, d), lambda b, i: (0, 0)),
        ],
        args=(x, mod, norm_g, w_in, w_out),
        out_specs=[row_spec],
        out_shapes=[jax.ShapeDtypeStruct(x.shape, F32)],
        casts=casts)
    return out, cast


def _proj_kernel(x_ref, mod_ref, kvmod_ref, ng_ref, kvg_ref, wkv_ref, wq_ref,
                 k_ref, v_ref, q_ref):
    d = x_ref.shape[1]

    @pl.when(pl.program_id(1) == 0)
    def _():
        k_ref[...] = jnp.zeros(k_ref.shape, BF16)
        v_ref[...] = jnp.zeros(v_ref.shape, BF16)

    @pl.when(pl.program_id(1) > 0)
    def _():
        xn = _unit_rms(x_ref[...])
        hkv = _norm_modulate(xn, kvg_ref[...], kvmod_ref[1:2, :], kvmod_ref[0:1, :]).astype(BF16)
        k_ref[...] = _dot(hkv, wkv_ref[:, 0:d]).astype(BF16)
        v_ref[...] = _dot(hkv, wkv_ref[:, d:2 * d]).astype(BF16)
        h = _norm_modulate(xn, ng_ref[0:1, :], mod_ref[1:2, :], mod_ref[0:1, :]).astype(BF16)
        q_ref[...] = (_dot(h, wq_ref[...]) * (LOG2_E * HEAD_DIM ** -0.5)).astype(BF16)


def _proj(x, mod, kvmod, norm_g, layer, kv_norm_g, w_kv, w_q, casts):
    bsz, s, d = x.shape
    tm = ROW_TILE
    assert tm == LEFT_ROWS
    row_spec = pl.BlockSpec((None, tm, d), lambda b, i: (b, jnp.maximum(i - 1, 0), 0))
    kv_spec = pl.BlockSpec((None, tm, d), lambda b, i: (b, i, 0))
    kv = jax.ShapeDtypeStruct((bsz, LEFT_ROWS + s, d), BF16)
    return _tiled_call(
        _proj_kernel, "qkv_proj", (bsz, s // tm + 1),
        in_specs=[
            row_spec,
            pl.BlockSpec((None, None, 6, d), lambda b, i: (layer, b, 0, 0)),
            pl.BlockSpec((None, None, 2, d), lambda b, i: (0, b, 0, 0)),
            _resident((None, 4, d), lambda b, i: (layer, 0, 0)),
            _resident((1, d), lambda b, i: (0, 0)),
            _resident((d, 2 * d), lambda b, i: (0, 0)),
            _resident((d, d), lambda b, i: (0, 0)),
        ],
        args=(x, mod, kvmod, norm_g, kv_norm_g, w_kv, w_q),
        out_specs=[kv_spec, kv_spec, row_spec],
        out_shapes=[kv, kv, jax.ShapeDtypeStruct(x.shape, BF16)],
        casts=casts)


def _bias_kernel(rb_ref, o_ref):
    n_heads = rb_ref.shape[0]
    n_tab = 2 * MAX_REL
    rb = rb_ref[:, 0:n_tab]
    hi = rb.astype(BF16)
    r1 = rb - hi.astype(F32)
    mid = r1.astype(BF16)
    lo = (r1 - mid.astype(F32)).astype(BF16)
    pieces = jnp.concatenate([hi, mid, lo], axis=0)
    rb_top = rb_ref[:, n_tab:n_tab + 1]
    u = lax.broadcasted_iota(jnp.int32, (1, WIN), 1)
    m = jnp.where(u >= WIN - CHUNK, u - WIN, u)
    idx = jnp.clip(N_LEFT_CHUNKS * CHUNK - m, -MAX_REL, MAX_REL) + MAX_REL
    row = lax.broadcasted_iota(jnp.int32, (n_tab, WIN), 0)
    onehot = jnp.where(row == idx, 1.0, 0.0).astype(BF16)
    t3 = _dot(pieces, onehot)
    w = t3[0:n_heads] + t3[n_heads:2 * n_heads] + t3[2 * n_heads:3 * n_heads]
    w = (w + jnp.where(idx == n_tab, rb_top, 0.0)) * LOG2_E
    col = lax.broadcasted_iota(jnp.int32, (CHUNK, WIN), 1)
    neg = jnp.finfo(F32).min
    for var in range(2):
        kk = col - var * CHUNK
        in_band = (kk >= 0) & (kk < BAND)
        for h in range(n_heads):
            t = pltpu.roll(jnp.broadcast_to(w[h:h + 1, :], (CHUNK, WIN)), var * CHUNK, 1,
                           stride=1, stride_axis=0)
            o_ref[var, h] = jnp.where(in_band, t, neg)[:, 0:SLAB]


def _bias_table(rel_bias, b_idx):
    n_heads, n_rel = rel_bias.shape[1:]
    return pl.pallas_call(
        _bias_kernel,
        grid=(1,),
        in_specs=[pl.BlockSpec((None, n_heads, n_rel), lambda i: (b_idx, 0, 0))],
        out_specs=pl.BlockSpec((2, n_heads, CHUNK, SLAB), lambda i: (0, 0, 0, 0)),
        out_shape=jax.ShapeDtypeStruct((2, n_heads, CHUNK, SLAB), F32),
        compiler_params=_params(1),
        name="rel_bias_table",
    )(rel_bias)


def _attn_kernel(x_ref, q_ref, kwin, vwin, bias_ref, wo_ref, mod_ref, ng_ref, o_ref, obuf):
    kwin, vwin = kwin.at[0], vwin.at[0]
    tm, d = q_ref.shape
    m_rows = 2 * GROUP * CHUNK
    low_head = lax.broadcasted_iota(jnp.int32, (CHUNK, V7X_LANES), 1) < HEAD_DIM
    neg = jnp.finfo(F32).min
    ones = jnp.ones((WIN, V7X_LANES), BF16)
    zcols = jnp.zeros((SOFTMAX_ROWS, WIN - SLAB), BF16)
    first_tile = pl.program_id(1) == 0
    col = lax.broadcasted_iota(jnp.int32, (m_rows, WIN), 1)
    left_of_seq = {w0: col < jnp.where(first_tile, LEFT_ROWS - w0, 0)
                   for w0 in range(0, LEFT_ROWS, GROUP * CHUNK)}

    def scores(g, p):
        w0 = g * GROUP * CHUNK
        ls = slice(p * V7X_LANES, (p + 1) * V7X_LANES)
        q2 = []
        for j in range(GROUP):
            qp = q_ref[w0 + j * CHUNK:w0 + (j + 1) * CHUNK, ls]
            zero = jnp.zeros_like(qp)
            q2 += [jnp.where(low_head, qp, zero), jnp.where(low_head, zero, qp)]
        s = lax.dot_general(jnp.concatenate(q2, axis=0), kwin[w0:w0 + WIN, ls],
                            (((1,), (1,)), ((), ())), preferred_element_type=F32)
        if w0 in left_of_seq:
            s = jnp.where(left_of_seq[w0], neg, s)
        return s

    def attend(g, p, s):
        w0 = g * GROUP * CHUNK
        ls = slice(p * V7X_LANES, (p + 1) * V7X_LANES)
        blocks = []
        for j in range(GROUP):
            t0 = (j // 2) * V7X_LANES
            for b0 in range(0, 2 * CHUNK, SOFTMAX_ROWS):
                r0 = 2 * j * CHUNK + b0
                sl = (s[r0:r0 + SOFTMAX_ROWS, t0:t0 + SLAB]
                      + bias_ref[j % 2, p, b0:b0 + SOFTMAX_ROWS, :])
                ex = jnp.exp2(sl - jnp.max(sl, axis=-1, keepdims=True)).astype(BF16)
                blocks.append(jnp.concatenate([ex, zcols] if t0 == 0 else [zcols, ex], axis=1))
        v_ones = jnp.concatenate([vwin[w0:w0 + WIN, ls], ones], axis=1)
        o2 = _dot(jnp.concatenate(blocks, axis=0), v_ones)
        o2 = o2[:, 0:V7X_LANES] / o2[:, V7X_LANES:2 * V7X_LANES]
        for j in range(GROUP):
            lo = o2[(2 * j) * CHUNK:(2 * j + 1) * CHUNK, :]
            hi = o2[(2 * j + 1) * CHUNK:(2 * j + 2) * CHUNK, :]
            obuf[w0 + j * CHUNK:w0 + (j + 1) * CHUNK, ls] = jnp.where(low_head, lo, hi).astype(BF16)

    bodies = [(g, p) for g in range(tm // (GROUP * CHUNK)) for p in range(d // V7X_LANES)]
    pending = {n: scores(*bodies[n]) for n in range(SCORES_AHEAD)}
    for n, body in enumerate(bodies):
        if n + SCORES_AHEAD < len(bodies):
            pending[n + SCORES_AHEAD] = scores(*bodies[n + SCORES_AHEAD])
        attend(*body, pending.pop(n))
    y = _dot(obuf[...], wo_ref[...])
    o_ref[...] = x_ref[...] + _gated_norm(y, ng_ref[1:2, :], mod_ref[2:3, :])


def _attn(x, q, k, v, bias, w_o, mod, norm_g, layer):
    bsz, s, d = x.shape
    tm = WIDE_ROW_TILE
    assert tm % (GROUP * CHUNK) == 0 and k.shape[1] == LEFT_ROWS + s
    n_pairs = d // V7X_LANES
    row_spec = pl.BlockSpec((None, tm, d), lambda b, i: (b, i, 0))
    win_spec = pl.BlockSpec((pl.Element(1), pl.Element(LEFT_ROWS + tm), pl.Element(d)),
                            lambda b, i: (b, i * tm, 0))
    return pl.pallas_call(
        _attn_kernel,
        grid=(bsz, s // tm),
        in_specs=[
            row_spec, row_spec, win_spec, win_spec,
            _resident((2, n_pairs, 2 * CHUNK, SLAB), lambda b, i: (0, 0, 0, 0)),
            _resident((d, d), lambda b, i: (0, 0)),
            pl.BlockSpec((None, None, 6, d), lambda b, i: (layer, b, 0, 0)),
            _resident((None, 4, d), lambda b, i: (layer, 0, 0)),
        ],
        out_specs=row_spec,
        out_shape=jax.ShapeDtypeStruct(x.shape, F32),
        scratch_shapes=[pltpu.VMEM((tm, d), BF16)],
        compiler_params=_params(2),
        name="band_attention",
    )(x, q, k, v, bias, w_o, mod, norm_g)


def kernel(x, c, mod_w, mod_b, norm_g, ffn_w_in, ffn_w_out, conv_w_in, conv_k, conv_w_out,
           kv_mod_w, kv_mod_b, kv_norm_g, w_kv, attn_w_q, attn_w_o, rel_bias):
    bsz, s, d = x.shape
    depth = mod_w.shape[0]
    n_a = conv_w_in.shape[0]
    n_heads = rel_bias.shape[1]
    assert d // n_heads == HEAD_DIM and s % WIDE_ROW_TILE == 0 and WIDE_ROW_TILE % ROW_TILE == 0

    mod = _modvec(c, mod_w, mod_b.reshape(depth, 1, 6 * d)).reshape(depth, bsz, 6, d)
    kvmod = _modvec(c, kv_mod_w[None], kv_mod_b.reshape(1, 1, 2 * d)).reshape(1, bsz, 2, d)

    conv_w_in, conv_w_out = conv_w_in.astype(BF16), conv_w_out.astype(BF16)
    kv_norm_g = kv_norm_g.reshape(1, d)
    w_kv = w_kv[None]

    k = v = None
    attn_w = None
    ffn_w = None
    for layer in range(depth):
        ffn_casts = [] if ffn_w else [(ffn_w_in, layer), (ffn_w_out, layer)]
        if layer < n_a:
            x, cast = _mixer(x, mod, norm_g, layer, conv_w_in, conv_k, conv_w_out, layer, ffn_casts)
        else:
            bi = layer - n_a
            if attn_w is None:
                attn_w = (w_kv[0].astype(BF16), attn_w_q[bi].astype(BF16), attn_w_o[bi].astype(BF16))
            w_kv_b, w_q_b, w_o_b = attn_w
            (k_new, v_new, q), cast = _proj(x, mod, kvmod, norm_g, layer, kv_norm_g,
                                            w_kv_b, w_q_b, ffn_casts)
            if layer == n_a:
                k, v = k_new, v_new
            bias = _bias_table(rel_bias, bi).reshape(2, n_heads // 2, 2 * CHUNK, SLAB)
            x = _attn(x, q, k, v, bias, w_o_b, mod, norm_g, layer)
        w_in, w_out = ffn_w or cast
        nxt = layer + 1
        next_casts = []
        if nxt < depth:
            next_casts = [(ffn_w_in, nxt), (ffn_w_out, nxt)]
            if nxt >= n_a:
                next_casts += [(w_kv, 0), (attn_w_q, nxt - n_a), (attn_w_o, nxt - n_a)]
        x, cast = _ffn(x, mod, norm_g, layer, w_in, w_out, next_casts)
        ffn_w = tuple(cast[0:2]) if nxt < depth else None
        attn_w = tuple(cast[2:5]) if len(cast) == 5 else None
    return x
```

```python
import functools
import math

import jax
import jax.numpy as jnp
from jax import lax
from jax.experimental import pallas as pl
from jax.experimental.pallas import tpu as pltpu

EPS = 1e-6
CHUNK = 64
N_LEFT_CHUNKS = 8
LEFT_ROWS = N_LEFT_CHUNKS * CHUNK
BAND = (N_LEFT_CHUNKS + 1) * CHUNK
SLAB = BAND + CHUNK
GROUP = 4
SCORES_AHEAD = 2
SOFTMAX_ROWS = 32
WIN = (GROUP + N_LEFT_CHUNKS) * CHUNK
HEAD_DIM = 64
MAX_REL = 2 * CHUNK
LOG2_E = math.log2(math.e)

V7X_LANES = 128
V7X_BF16_SUBLANES = 16
V7X_MXU_COLS = 256
ROW_TILE = 512
WIDE_ROW_TILE = 1024
SUB_ROWS = 256
V7X_VMEM_LIMIT = 56 * 1024 * 1024

BF16 = jnp.bfloat16
F32 = jnp.float32


def _dot(a, b):
    return jnp.dot(a, b, preferred_element_type=F32)


def _unit_rms(x):
    return x * lax.rsqrt(jnp.mean(x * x, axis=-1, keepdims=True) + EPS)


def _norm_modulate(xn, g, scale, shift):
    return xn * (g * (1.0 + scale)) + shift


def _gated_norm(y, g, gate):
    return _unit_rms(y) * (gate * g)


def _resident(block_shape, index_map):
    return pl.BlockSpec(block_shape, index_map, pipeline_mode=pl.Buffered(1))


def _params(n_grid_dims):
    return pltpu.CompilerParams(
        dimension_semantics=("arbitrary",) * n_grid_dims,
        vmem_limit_bytes=V7X_VMEM_LIMIT)


def _with_casts(body, n_in, n_out, n_cast):
    def wrapped(*refs):
        srcs = refs[n_in:n_in + n_cast]
        outs_end = n_in + n_cast + n_out
        dsts = refs[outs_end:outs_end + n_cast]
        for src, dst in zip(srcs, dsts):
            dst[...] = src[...].astype(BF16)
        body(*refs[:n_in], *refs[n_in + n_cast:outs_end], *refs[outs_end + n_cast:])
    return wrapped


def _tiled_call(body, name, grid, in_specs, args, out_specs, out_shapes, scratch=(), casts=()):
    bsz, n_tiles = grid
    c_in, c_out, c_shapes, c_args = [], [], [], []
    for w, idx in casts:
        r, c = w.shape[1:]
        rows, cols = r // n_tiles, c // bsz
        assert rows * n_tiles == r and rows % V7X_BF16_SUBLANES == 0
        assert cols * bsz == c and cols % V7X_LANES == 0
        c_in.append(pl.BlockSpec((None, rows, cols), lambda b, i, idx=idx: (idx, i, b)))
        c_out.append(pl.BlockSpec((rows, cols), lambda b, i: (i, b)))
        c_shapes.append(jax.ShapeDtypeStruct((r, c), BF16))
        c_args.append(w)
    res = pl.pallas_call(
        _with_casts(body, len(in_specs), len(out_specs), len(casts)),
        grid=grid,
        in_specs=list(in_specs) + c_in,
        out_specs=list(out_specs) + c_out,
        out_shape=list(out_shapes) + c_shapes,
        scratch_shapes=list(scratch),
        compiler_params=_params(len(grid)),
        name=name,
    )(*args, *c_args)
    return res[:len(out_specs)], res[len(out_specs):]


def _modvec_kernel(c_ref, w_ref, b_ref, o_ref):
    s = jax.nn.silu(c_ref[...]).astype(BF16)
    o_ref[...] = _dot(s, w_ref[...].astype(BF16)) + b_ref[...]


def _modvec(c, w, b, col_tile=1024):
    n_layers, d, n = w.shape
    bsz = c.shape[0]
    return pl.pallas_call(
        _modvec_kernel,
        grid=(n_layers, n // col_tile),
        in_specs=[
            pl.BlockSpec((bsz, d), lambda l, j: (0, 0)),
            pl.BlockSpec((None, d, col_tile), lambda l, j: (l, 0, j)),
            pl.BlockSpec((None, 1, col_tile), lambda l, j: (l, 0, j)),
        ],
        out_specs=pl.BlockSpec((None, bsz, col_tile), lambda l, j: (l, 0, j)),
        out_shape=jax.ShapeDtypeStruct((n_layers, bsz, n), F32),
        compiler_params=_params(2),
        name="modvec",
    )(c, w, b)


def _mixer_kernel(x_ref, mod_ref, ng_ref, win_ref, ck_ref, wout_ref, o_ref, zbuf, *, n_sub):
    tm, d = x_ref.shape
    sh1, sc1, g1 = mod_ref[0:1, :], mod_ref[1:2, :], mod_ref[2:3, :]

    @pl.when(pl.program_id(1) == 0)
    def _():
        zbuf[0:8, :] = jnp.zeros((8, d), F32)

    rows = tm // n_sub
    for r0 in range(0, tm, rows):
        x = x_ref[r0:r0 + rows, :]
        h = _norm_modulate(_unit_rms(x), ng_ref[0:1, :], sc1, sh1).astype(BF16)
        us = []
        for c0 in range(0, d, V7X_MXU_COLS):
            cs = slice(c0, c0 + V7X_MXU_COLS)
            z = (_dot(h, win_ref[:, d + c0:d + c0 + V7X_MXU_COLS])
                 * _dot(h, win_ref[:, 2 * d + c0:2 * d + c0 + V7X_MXU_COLS]))
            zbuf[8 + r0:8 + r0 + rows, cs] = z
            conv = (ck_ref[0:1, cs] * zbuf[6 + r0:6 + r0 + rows, cs]
                    + ck_ref[1:2, cs] * zbuf[7 + r0:7 + r0 + rows, cs]
                    + ck_ref[2:3, cs] * z)
            us.append((_dot(h, win_ref[:, cs]) * conv).astype(BF16))
        y = _dot(jnp.concatenate(us, axis=1), wout_ref[...])
        o_ref[r0:r0 + rows, :] = x + _gated_norm(y, ng_ref[1:2, :], g1)
    zbuf[0:8, :] = zbuf[tm:tm + 8, :]


def _mixer(x, mod, norm_g, layer, w_in, conv_k, w_out, a_idx, casts):
    bsz, s, d = x.shape
    tm = WIDE_ROW_TILE
    row_spec = pl.BlockSpec((None, tm, d), lambda b, i: (b, i, 0))
    (out,), cast = _tiled_call(
        functools.partial(_mixer_kernel, n_sub=tm // SUB_ROWS), "conv_mixer", (bsz, s // tm),
        in_specs=[
            row_spec,
            pl.BlockSpec((None, None, 6, d), lambda b, i: (layer, b, 0, 0)),
            _resident((None, 4, d), lambda b, i: (layer, 0, 0)),
            _resident((None, d, 3 * d), lambda b, i: (a_idx, 0, 0)),
            _resident((None, 3, d), lambda b, i: (a_idx, 0, 0)),
            _resident((None, d, d), lambda b, i: (a_idx, 0, 0)),
        ],
        args=(x, mod, norm_g, w_in, conv_k, w_out),
        out_specs=[row_spec],
        out_shapes=[jax.ShapeDtypeStruct(x.shape, F32)],
        scratch=[pltpu.VMEM((tm + 8, d), F32)],
        casts=casts)
    return out, cast


def _ffn_kernel(x_ref, mod_ref, ng_ref, win_ref, wout_ref, o_ref, *, n_sub):
    sh2, sc2, g2 = mod_ref[3:4, :], mod_ref[4:5, :], mod_ref[5:6, :]
    f = wout_ref.shape[0]
    rows = x_ref.shape[0] // n_sub
    for r in range(n_sub):
        rs = slice(r * rows, (r + 1) * rows)
        x = x_ref[rs, :]
        h = _norm_modulate(_unit_rms(x), ng_ref[2:3, :], sc2, sh2).astype(BF16)
        g = _dot(h, win_ref[:, 0:f])
        u = _dot(h, win_ref[:, f:2 * f])
        y = _dot((jax.nn.silu(g) * u).astype(BF16), wout_ref[...])
        o_ref[rs, :] = x + _gated_norm(y, ng_ref[3:4, :], g2)


def _ffn(x, mod, norm_g, layer, w_in, w_out, casts):
    bsz, s, d = x.shape
    f = w_out.shape[0]
    tm = WIDE_ROW_TILE
    row_spec = pl.BlockSpec((None, tm, d), lambda b, i: (b, i, 0))
    (out,), cast = _tiled_call(
        functools.partial(_ffn_kernel, n_sub=WIDE_ROW_TILE // SUB_ROWS), "swiglu_ffn", (bsz, s // tm),
        in_specs=[
            row_spec,
            pl.BlockSpec((None, None, 6, d), lambda b, i: (layer, b, 0, 0)),
            _resident((None, 4, d), lambda b, i: (layer, 0, 0)),
            _resident((d, 2 * f), lambda b, i: (0, 0)),
            _resident((f, d), lambda b, i: (0, 0)),
        ],
        args=(x, mod, norm_g, w_in, w_out),
        out_specs=[row_spec],
        out_shapes=[jax.ShapeDtypeStruct(x.shape, F32)],
        casts=casts)
    return out, cast


def _proj_kernel(x_ref, mod_ref, kvmod_ref, ng_ref, kvg_ref, wkv_ref, wq_ref,
                 k_ref, v_ref, q_ref):
    d = x_ref.shape[1]

    @pl.when(pl.program_id(1) == 0)
    def _():
        k_ref[...] = jnp.zeros(k_ref.shape, BF16)
        v_ref[...] = jnp.zeros(v_ref.shape, BF16)

    @pl.when(pl.program_id(1) > 0)
    def _():
        xn = _unit_rms(x_ref[...])
        hkv = _norm_modulate(xn, kvg_ref[...], kvmod_ref[1:2, :], kvmod_ref[0:1, :]).astype(BF16)
        k_ref[...] = _dot(hkv, wkv_ref[:, 0:d]).astype(BF16)
        v_ref[...] = _dot(hkv, wkv_ref[:, d:2 * d]).astype(BF16)
        h = _norm_modulate(xn, ng_ref[0:1, :], mod_ref[1:2, :], mod_ref[0:1, :]).astype(BF16)
        q_ref[...] = (_dot(h, wq_ref[...]) * (LOG2_E * HEAD_DIM ** -0.5)).astype(BF16)


def _proj(x, mod, kvmod, norm_g, layer, kv_norm_g, w_kv, w_q, casts):
    bsz, s, d = x.shape
    tm = ROW_TILE
    assert tm == LEFT_ROWS
    row_spec = pl.BlockSpec((None, tm, d), lambda b, i: (b, jnp.maximum(i - 1, 0), 0))
    kv_spec = pl.BlockSpec((None, tm, d), lambda b, i: (b, i, 0))
    kv = jax.ShapeDtypeStruct((bsz, LEFT_ROWS + s, d), BF16)
    return _tiled_call(
        _proj_kernel, "qkv_proj", (bsz, s // tm + 1),
        in_specs=[
            row_spec,
            pl.BlockSpec((None, None, 6, d), lambda b, i: (layer, b, 0, 0)),
            pl.BlockSpec((None, None, 2, d), lambda b, i: (0, b, 0, 0)),
            _resident((None, 4, d), lambda b, i: (layer, 0, 0)),
            _resident((1, d), lambda b, i: (0, 0)),
            _resident((d, 2 * d), lambda b, i: (0, 0)),
            _resident((d, d), lambda b, i: (0, 0)),
        ],
        args=(x, mod, kvmod, norm_g, kv_norm_g, w_kv, w_q),
        out_specs=[kv_spec, kv_spec, row_spec],
        out_shapes=[kv, kv, jax.ShapeDtypeStruct(x.shape, BF16)],
        casts=casts)


def _bias_kernel(rb_ref, o_ref):
    n_heads = rb_ref.shape[0]
    n_tab = 2 * MAX_REL
    rb = rb_ref[:, 0:n_tab]
    hi = rb.astype(BF16)
    r1 = rb - hi.astype(F32)
    mid = r1.astype(BF16)
    lo = (r1 - mid.astype(F32)).astype(BF16)
    pieces = jnp.concatenate([hi, mid, lo], axis=0)
    rb_top = rb_ref[:, n_tab:n_tab + 1]
    u = lax.broadcasted_iota(jnp.int32, (1, WIN), 1)
    m = jnp.where(u >= WIN - CHUNK, u - WIN, u)
    idx = jnp.clip(N_LEFT_CHUNKS * CHUNK - m, -MAX_REL, MAX_REL) + MAX_REL
    row = lax.broadcasted_iota(jnp.int32, (n_tab, WIN), 0)
    onehot = jnp.where(row == idx, 1.0, 0.0).astype(BF16)
    t3 = _dot(pieces, onehot)
    w = t3[0:n_heads] + t3[n_heads:2 * n_heads] + t3[2 * n_heads:3 * n_heads]
    w = (w + jnp.where(idx == n_tab, rb_top, 0.0)) * LOG2_E
    col = lax.broadcasted_iota(jnp.int32, (CHUNK, WIN), 1)
    neg = jnp.finfo(F32).min
    for var in range(2):
        kk = col - var * CHUNK
        in_band = (kk >= 0) & (kk < BAND)
        for h in range(n_heads):
            t = pltpu.roll(jnp.broadcast_to(w[h:h + 1, :], (CHUNK, WIN)), var * CHUNK, 1,
                           stride=1, stride_axis=0)
            o_ref[var, h] = jnp.where(in_band, t, neg)[:, 0:SLAB]


def _bias_table(rel_bias, b_idx):
    n_heads, n_rel = rel_bias.shape[1:]
    return pl.pallas_call(
        _bias_kernel,
        grid=(1,),
        in_specs=[pl.BlockSpec((None, n_heads, n_rel), lambda i: (b_idx, 0, 0))],
        out_specs=pl.BlockSpec((2, n_heads, CHUNK, SLAB), lambda i: (0, 0, 0, 0)),
        out_shape=jax.ShapeDtypeStruct((2, n_heads, CHUNK, SLAB), F32),
        compiler_params=_params(1),
        name="rel_bias_table",
    )(rel_bias)


def _attn_kernel(x_ref, q_ref, kwin, vwin, bias_ref, wo_ref, mod_ref, ng_ref, o_ref, obuf):
    kwin, vwin = kwin.at[0], vwin.at[0]
    tm, d = q_ref.shape
    m_rows = 2 * GROUP * CHUNK
    low_head = lax.broadcasted_iota(jnp.int32, (CHUNK, V7X_LANES), 1) < HEAD_DIM
    neg = jnp.finfo(F32).min
    ones = jnp.ones((WIN, V7X_LANES), BF16)
    zcols = jnp.zeros((SOFTMAX_ROWS, WIN - SLAB), BF16)
    first_tile = pl.program_id(1) == 0
    col = lax.broadcasted_iota(jnp.int32, (m_rows, WIN), 1)
    left_of_seq = {w0: col < jnp.where(first_tile, LEFT_ROWS - w0, 0)
                   for w0 in range(0, LEFT_ROWS, GROUP * CHUNK)}

    def scores(g, p):
        w0 = g * GROUP * CHUNK
        ls = slice(p * V7X_LANES, (p + 1) * V7X_LANES)
        q2 = []
        for j in range(GROUP):
            qp = q_ref[w0 + j * CHUNK:w0 + (j + 1) * CHUNK, ls]
            zero = jnp.zeros_like(qp)
            q2 += [jnp.where(low_head, qp, zero), jnp.where(low_head, zero, qp)]
        s = lax.dot_general(jnp.concatenate(q2, axis=0), kwin[w0:w0 + WIN, ls],
                            (((1,), (1,)), ((), ())), preferred_element_type=F32)
        if w0 in left_of_seq:
            s = jnp.where(left_of_seq[w0], neg, s)
        return s

    def attend(g, p, s):
        w0 = g * GROUP * CHUNK
        ls = slice(p * V7X_LANES, (p + 1) * V7X_LANES)
        blocks = []
        for j in range(GROUP):
            t0 = (j // 2) * V7X_LANES
            for b0 in range(0, 2 * CHUNK, SOFTMAX_ROWS):
                r0 = 2 * j * CHUNK + b0
                sl = (s[r0:r0 + SOFTMAX_ROWS, t0:t0 + SLAB]
                      + bias_ref[j % 2, p, b0:b0 + SOFTMAX_ROWS, :])
                ex = jnp.exp2(sl - jnp.max(sl, axis=-1, keepdims=True)).astype(BF16)
                blocks.append(jnp.concatenate([ex, zcols] if t0 == 0 else [zcols, ex], axis=1))
        v_ones = jnp.concatenate([vwin[w0:w0 + WIN, ls], ones], axis=1)
        o2 = _dot(jnp.concatenate(blocks, axis=0), v_ones)
        o2 = o2[:, 0:V7X_LANES] / o2[:, V7X_LANES:2 * V7X_LANES]
        for j in range(GROUP):
            lo = o2[(2 * j) * CHUNK:(2 * j + 1) * CHUNK, :]
            hi = o2[(2 * j + 1) * CHUNK:(2 * j + 2) * CHUNK, :]
            obuf[w0 + j * CHUNK:w0 + (j + 1) * CHUNK, ls] = jnp.where(low_head, lo, hi).astype(BF16)

    bodies = [(g, p) for g in range(tm // (GROUP * CHUNK)) for p in range(d // V7X_LANES)]
    pending = {n: scores(*bodies[n]) for n in range(SCORES_AHEAD)}
    for n, body in enumerate(bodies):
        if n + SCORES_AHEAD < len(bodies):
            pending[n + SCORES_AHEAD] = scores(*bodies[n + SCORES_AHEAD])
        attend(*body, pending.pop(n))
    y = _dot(obuf[...], wo_ref[...])
    o_ref[...] = x_ref[...] + _gated_norm(y, ng_ref[1:2, :], mod_ref[2:3, :])


def _attn(x, q, k, v, bias, w_o, mod, norm_g, layer):
    bsz, s, d = x.shape
    tm = WIDE_ROW_TILE
    assert tm % (GROUP * CHUNK) == 0 and k.shape[1] == LEFT_ROWS + s
    n_pairs = d // V7X_LANES
    row_spec = pl.BlockSpec((None, tm, d), lambda b, i: (b, i, 0))
    win_spec = pl.BlockSpec((pl.Element(1), pl.Element(LEFT_ROWS + tm), pl.Element(d)),
                            lambda b, i: (b, i * tm, 0))
    return pl.pallas_call(
        _attn_kernel,
        grid=(bsz, s // tm),
        in_specs=[
            row_spec, row_spec, win_spec, win_spec,
            _resident((2, n_pairs, 2 * CHUNK, SLAB), lambda b, i: (0, 0, 0, 0)),
            _resident((d, d), lambda b, i: (0, 0)),
            pl.BlockSpec((None, None, 6, d), lambda b, i: (layer, b, 0, 0)),
            _resident((None, 4, d), lambda b, i: (layer, 0, 0)),
        ],
        out_specs=row_spec,
        out_shape=jax.ShapeDtypeStruct(x.shape, F32),
        scratch_shapes=[pltpu.VMEM((tm, d), BF16)],
        compiler_params=_params(2),
        name="band_attention",
    )(x, q, k, v, bias, w_o, mod, norm_g)


def kernel(x, c, mod_w, mod_b, norm_g, ffn_w_in, ffn_w_out, conv_w_in, conv_k, conv_w_out,
           kv_mod_w, kv_mod_b, kv_norm_g, w_kv, attn_w_q, attn_w_o, rel_bias):
    bsz, s, d = x.shape
    depth = mod_w.shape[0]
    n_a = conv_w_in.shape[0]
    n_heads = rel_bias.shape[1]
    assert d // n_heads == HEAD_DIM and s % WIDE_ROW_TILE == 0 and WIDE_ROW_TILE % ROW_TILE == 0

    mod = _modvec(c, mod_w, mod_b.reshape(depth, 1, 6 * d)).reshape(depth, bsz, 6, d)
    kvmod = _modvec(c, kv_mod_w[None], kv_mod_b.reshape(1, 1, 2 * d)).reshape(1, bsz, 2, d)

    conv_w_in, conv_w_out = conv_w_in.astype(BF16), conv_w_out.astype(BF16)
    kv_norm_g = kv_norm_g.reshape(1, d)
    w_kv = w_kv[None]

    k = v = None
    attn_w = None
    ffn_w = None
    for layer in range(depth):
        ffn_casts = [] if ffn_w else [(ffn_w_in, layer), (ffn_w_out, layer)]
        if layer < n_a:
            x, cast = _mixer(x, mod, norm_g, layer, conv_w_in, conv_k, conv_w_out, layer, ffn_casts)
        else:
            bi = layer - n_a
            if attn_w is None:
                attn_w = (w_kv[0].astype(BF16), attn_w_q[bi].astype(BF16), attn_w_o[bi].astype(BF16))
            w_kv_b, w_q_b, w_o_b = attn_w
            (k_new, v_new, q), cast = _proj(x, mod, kvmod, norm_g, layer, kv_norm_g,
                                            w_kv_b, w_q_b, ffn_casts)
            if layer == n_a:
                k, v = k_new, v_new
            bias = _bias_table(rel_bias, bi).reshape(2, n_heads // 2, 2 * CHUNK, SLAB)
            x = _attn(x, q, k, v, bias, w_o_b, mod, norm_g, layer)
        w_in, w_out = ffn_w or cast
        nxt = layer + 1
        next_casts = []
        if nxt < depth:
            next_casts = [(ffn_w_in, nxt), (ffn_w_out, nxt)]
            if nxt >= n_a:
                next_casts += [(w_kv, 0), (attn_w_q, nxt - n_a), (attn_w_o, nxt - n_a)]
        x, cast = _ffn(x, mod, norm_g, layer, w_in, w_out, next_casts)
        ffn_w = tuple(cast[0:2]) if nxt < depth else None
        attn_w = tuple(cast[2:5]) if len(cast) == 5 else None
    return x
```

```python
import functools
import math

import jax
import jax.numpy as jnp
from jax import lax
from jax.experimental import pallas as pl
from jax.experimental.pallas import tpu as pltpu

EPS = 1e-6
CHUNK = 64
N_LEFT_CHUNKS = 8
LEFT_ROWS = N_LEFT_CHUNKS * CHUNK
BAND = (N_LEFT_CHUNKS + 1) * CHUNK
SLAB = BAND + CHUNK
GROUP = 4
SCORES_AHEAD = 2
SOFTMAX_ROWS = 32
WIN = (GROUP + N_LEFT_CHUNKS) * CHUNK
HEAD_DIM = 64
MAX_REL = 2 * CHUNK
LOG2_E = math.log2(math.e)

V7X_LANES = 128
V7X_BF16_SUBLANES = 16
V7X_MXU_COLS = 256
ROW_TILE = 512
WIDE_ROW_TILE = 1024
MIXER_SUB_ROWS = 512
FFN_SUB_ROWS = 256
V7X_VMEM_LIMIT = 56 * 1024 * 1024

BF16 = jnp.bfloat16
F32 = jnp.float32


def _dot(a, b):
    return jnp.dot(a, b, preferred_element_type=F32)


def _unit_rms(x):
    return x * lax.rsqrt(jnp.mean(x * x, axis=-1, keepdims=True) + EPS)


def _norm_modulate(xn, g, scale, shift):
    return xn * (g * (1.0 + scale)) + shift


def _gated_norm(y, g, gate):
    return _unit_rms(y) * (gate * g)


def _resident(block_shape, index_map):
    return pl.BlockSpec(block_shape, index_map, pipeline_mode=pl.Buffered(1))


def _params(n_grid_dims):
    return pltpu.CompilerParams(
        dimension_semantics=("arbitrary",) * n_grid_dims,
        vmem_limit_bytes=V7X_VMEM_LIMIT)


def _with_casts(body, n_in, n_out, n_cast):
    def wrapped(*refs):
        srcs = refs[n_in:n_in + n_cast]
        outs_end = n_in + n_cast + n_out
        dsts = refs[outs_end:outs_end + n_cast]
        for src, dst in zip(srcs, dsts):
            dst[...] = src[...].astype(BF16)
        body(*refs[:n_in], *refs[n_in + n_cast:outs_end], *refs[outs_end + n_cast:])
    return wrapped


def _tiled_call(body, name, grid, in_specs, args, out_specs, out_shapes, scratch=(), casts=()):
    bsz, n_tiles = grid
    c_in, c_out, c_shapes, c_args = [], [], [], []
    for w, idx in casts:
        r, c = w.shape[1:]
        rows, cols = r // n_tiles, c // bsz
        assert rows * n_tiles == r and rows % V7X_BF16_SUBLANES == 0
        assert cols * bsz == c and cols % V7X_LANES == 0
        c_in.append(pl.BlockSpec((None, rows, cols), lambda b, i, idx=idx: (idx, i, b)))
        c_out.append(pl.BlockSpec((rows, cols), lambda b, i: (i, b)))
        c_shapes.append(jax.ShapeDtypeStruct((r, c), BF16))
        c_args.append(w)
    res = pl.pallas_call(
        _with_casts(body, len(in_specs), len(out_specs), len(casts)),
        grid=grid,
        in_specs=list(in_specs) + c_in,
        out_specs=list(out_specs) + c_out,
        out_shape=list(out_shapes) + c_shapes,
        scratch_shapes=list(scratch),
        compiler_params=_params(len(grid)),
        name=name,
    )(*args, *c_args)
    return res[:len(out_specs)], res[len(out_specs):]


def _modvec_kernel(c_ref, w_ref, b_ref, o_ref):
    s = jax.nn.silu(c_ref[...]).astype(BF16)
    o_ref[...] = _dot(s, w_ref[...].astype(BF16)) + b_ref[...]


def _modvec(c, w, b, col_tile=1024):
    n_layers, d, n = w.shape
    bsz = c.shape[0]
    return pl.pallas_call(
        _modvec_kernel,
        grid=(n_layers, n // col_tile),
        in_specs=[
            pl.BlockSpec((bsz, d), lambda l, j: (0, 0)),
            pl.BlockSpec((None, d, col_tile), lambda l, j: (l, 0, j)),
            pl.BlockSpec((None, 1, col_tile), lambda l, j: (l, 0, j)),
        ],
        out_specs=pl.BlockSpec((None, bsz, col_tile), lambda l, j: (l, 0, j)),
        out_shape=jax.ShapeDtypeStruct((n_layers, bsz, n), F32),
        compiler_params=_params(2),
        name="modvec",
    )(c, w, b)


def _mixer_kernel(x_ref, mod_ref, ng_ref, win_ref, ck_ref, wout_ref, o_ref, zbuf, *, n_sub):
    tm, d = x_ref.shape
    sh1, sc1, g1 = mod_ref[0:1, :], mod_ref[1:2, :], mod_ref[2:3, :]

    @pl.when(pl.program_id(1) == 0)
    def _():
        zbuf[0:8, :] = jnp.zeros((8, d), F32)

    rows = tm // n_sub
    for r0 in range(0, tm, rows):
        x = x_ref[r0:r0 + rows, :]
        h = _norm_modulate(_unit_rms(x), ng_ref[0:1, :], sc1, sh1).astype(BF16)
        us = []
        for c0 in range(0, d, V7X_MXU_COLS):
            cs = slice(c0, c0 + V7X_MXU_COLS)
            z = (_dot(h, win_ref[:, d + c0:d + c0 + V7X_MXU_COLS])
                 * _dot(h, win_ref[:, 2 * d + c0:2 * d + c0 + V7X_MXU_COLS]))
            zbuf[8 + r0:8 + r0 + rows, cs] = z
            conv = (ck_ref[0:1, cs] * zbuf[6 + r0:6 + r0 + rows, cs]
                    + ck_ref[1:2, cs] * zbuf[7 + r0:7 + r0 + rows, cs]
                    + ck_ref[2:3, cs] * z)
            us.append((_dot(h, win_ref[:, cs]) * conv).astype(BF16))
        y = _dot(jnp.concatenate(us, axis=1), wout_ref[...])
        o_ref[r0:r0 + rows, :] = x + _gated_norm(y, ng_ref[1:2, :], g1)
    zbuf[0:8, :] = zbuf[tm:tm + 8, :]


def _mixer(x, mod, norm_g, layer, w_in, conv_k, w_out, a_idx, casts):
    bsz, s, d = x.shape
    tm = WIDE_ROW_TILE
    row_spec = pl.BlockSpec((None, tm, d), lambda b, i: (b, i, 0))
    (out,), cast = _tiled_call(
        functools.partial(_mixer_kernel, n_sub=tm // MIXER_SUB_ROWS), "conv_mixer", (bsz, s // tm),
        in_specs=[
            row_spec,
            pl.BlockSpec((None, None, 6, d), lambda b, i: (layer, b, 0, 0)),
            _resident((None, 4, d), lambda b, i: (layer, 0, 0)),
            _resident((None, d, 3 * d), lambda b, i: (a_idx, 0, 0)),
            _resident((None, 3, d), lambda b, i: (a_idx, 0, 0)),
            _resident((None, d, d), lambda b, i: (a_idx, 0, 0)),
        ],
        args=(x, mod, norm_g, w_in, conv_k, w_out),
        out_specs=[row_spec],
        out_shapes=[jax.ShapeDtypeStruct(x.shape, F32)],
        scratch=[pltpu.VMEM((tm + 8, d), F32)],
        casts=casts)
    return out, cast


def _ffn_kernel(x_ref, mod_ref, ng_ref, win_ref, wout_ref, o_ref, *, n_sub):
    sh2, sc2, g2 = mod_ref[3:4, :], mod_ref[4:5, :], mod_ref[5:6, :]
    f = wout_ref.shape[0]
    rows = x_ref.shape[0] // n_sub
    for r in range(n_sub):
        rs = slice(r * rows, (r + 1) * rows)
        x = x_ref[rs, :]
        h = _norm_modulate(_unit_rms(x), ng_ref[2:3, :], sc2, sh2).astype(BF16)
        g = _dot(h, win_ref[:, 0:f])
        u = _dot(h, win_ref[:, f:2 * f])
        y = _dot((jax.nn.silu(g) * u).astype(BF16), wout_ref[...])
        o_ref[rs, :] = x + _gated_norm(y, ng_ref[3:4, :], g2)


def _ffn(x, mod, norm_g, layer, w_in, w_out, casts):
    bsz, s, d = x.shape
    f = w_out.shape[0]
    tm = WIDE_ROW_TILE
    row_spec = pl.BlockSpec((None, tm, d), lambda b, i: (b, i, 0))
    (out,), cast = _tiled_call(
        functools.partial(_ffn_kernel, n_sub=tm // FFN_SUB_ROWS), "swiglu_ffn", (bsz, s // tm),
        in_specs=[
            row_spec,
            pl.BlockSpec((None, None, 6, d), lambda b, i: (layer, b, 0, 0)),
            _resident((None, 4, d), lambda b, i: (layer, 0, 0)),
            _resident((d, 2 * f), lambda b, i: (0, 0)),
            _resident((f, d), lambda b, i: (0, 0)),
        ],
        args=(x, mod, norm_g, w_in, w_out),
        out_specs=[row_spec],
        out_shapes=[jax.ShapeDtypeStruct(x.shape, F32)],
        casts=casts)
    return out, cast


def _proj_kernel(x_ref, mod_ref, kvmod_ref, ng_ref, kvg_ref, wkv_ref, wq_ref,
                 k_ref, v_ref, q_ref):
    d = x_ref.shape[1]

    @pl.when(pl.program_id(1) == 0)
    def _():
        k_ref[...] = jnp.zeros(k_ref.shape, BF16)
        v_ref[...] = jnp.zeros(v_ref.shape, BF16)

    @pl.when(pl.program_id(1) > 0)
    def _():
        xn = _unit_rms(x_ref[...])
        hkv = _norm_modulate(xn, kvg_ref[...], kvmod_ref[1:2, :], kvmod_ref[0:1, :]).astype(BF16)
        k_ref[...] = _dot(hkv, wkv_ref[:, 0:d]).astype(BF16)
        v_ref[...] = _dot(hkv, wkv_ref[:, d:2 * d]).astype(BF16)
        h = _norm_modulate(xn, ng_ref[0:1, :], mod_ref[1:2, :], mod_ref[0:1, :]).astype(BF16)
        q_ref[...] = (_dot(h, wq_ref[...]) * (LOG2_E * HEAD_DIM ** -0.5)).astype(BF16)


def _proj(x, mod, kvmod, norm_g, layer, kv_norm_g, w_kv, w_q, casts):
    bsz, s, d = x.shape
    tm = ROW_TILE
    assert tm == LEFT_ROWS
    row_spec = pl.BlockSpec((None, tm, d), lambda b, i: (b, jnp.maximum(i - 1, 0), 0))
    kv_spec = pl.BlockSpec((None, tm, d), lambda b, i: (b, i, 0))
    kv = jax.ShapeDtypeStruct((bsz, LEFT_ROWS + s, d), BF16)
    return _tiled_call(
        _proj_kernel, "qkv_proj", (bsz, s // tm + 1),
        in_specs=[
            row_spec,
            pl.BlockSpec((None, None, 6, d), lambda b, i: (layer, b, 0, 0)),
            pl.BlockSpec((None, None, 2, d), lambda b, i: (0, b, 0, 0)),
            _resident((None, 4, d), lambda b, i: (layer, 0, 0)),
            _resident((1, d), lambda b, i: (0, 0)),
            _resident((d, 2 * d), lambda b, i: (0, 0)),
            _resident((d, d), lambda b, i: (0, 0)),
        ],
        args=(x, mod, kvmod, norm_g, kv_norm_g, w_kv, w_q),
        out_specs=[kv_spec, kv_spec, row_spec],
        out_shapes=[kv, kv, jax.ShapeDtypeStruct(x.shape, BF16)],
        casts=casts)


def _bias_kernel(rb_ref, o_ref):
    n_heads = rb_ref.shape[0]
    n_tab = 2 * MAX_REL
    rb = rb_ref[:, 0:n_tab]
    hi = rb.astype(BF16)
    r1 = rb - hi.astype(F32)
    mid = r1.astype(BF16)
    lo = (r1 - mid.astype(F32)).astype(BF16)
    pieces = jnp.concatenate([hi, mid, lo], axis=0)
    rb_top = rb_ref[:, n_tab:n_tab + 1]
    u = lax.broadcasted_iota(jnp.int32, (1, WIN), 1)
    m = jnp.where(u >= WIN - CHUNK, u - WIN, u)
    idx = jnp.clip(N_LEFT_CHUNKS * CHUNK - m, -MAX_REL, MAX_REL) + MAX_REL
    row = lax.broadcasted_iota(jnp.int32, (n_tab, WIN), 0)
    onehot = jnp.where(row == idx, 1.0, 0.0).astype(BF16)
    t3 = _dot(pieces, onehot)
    w = t3[0:n_heads] + t3[n_heads:2 * n_heads] + t3[2 * n_heads:3 * n_heads]
    w = (w + jnp.where(idx == n_tab, rb_top, 0.0)) * LOG2_E
    col = lax.broadcasted_iota(jnp.int32, (CHUNK, WIN), 1)
    neg = jnp.finfo(F32).min
    for var in range(2):
        kk = col - var * CHUNK
        in_band = (kk >= 0) & (kk < BAND)
        for h in range(n_heads):
            t = pltpu.roll(jnp.broadcast_to(w[h:h + 1, :], (CHUNK, WIN)), var * CHUNK, 1,
                           stride=1, stride_axis=0)
            o_ref[var, h] = jnp.where(in_band, t, neg)[:, 0:SLAB]


def _bias_table(rel_bias, b_idx):
    n_heads, n_rel = rel_bias.shape[1:]
    return pl.pallas_call(
        _bias_kernel,
        grid=(1,),
        in_specs=[pl.BlockSpec((None, n_heads, n_rel), lambda i: (b_idx, 0, 0))],
        out_specs=pl.BlockSpec((2, n_heads, CHUNK, SLAB), lambda i: (0, 0, 0, 0)),
        out_shape=jax.ShapeDtypeStruct((2, n_heads, CHUNK, SLAB), F32),
        compiler_params=_params(1),
        name="rel_bias_table",
    )(rel_bias)


def _attn_kernel(x_ref, q_ref, kwin, vwin, bias_ref, wo_ref, mod_ref, ng_ref, o_ref, obuf):
    kwin, vwin = kwin.at[0], vwin.at[0]
    tm, d = q_ref.shape
    m_rows = 2 * GROUP * CHUNK
    low_head = lax.broadcasted_iota(jnp.int32, (CHUNK, V7X_LANES), 1) < HEAD_DIM
    neg = jnp.finfo(F32).min
    ones = jnp.ones((WIN, V7X_LANES), BF16)
    zcols = jnp.zeros((SOFTMAX_ROWS, WIN - SLAB), BF16)
    first_tile = pl.program_id(1) == 0
    col = lax.broadcasted_iota(jnp.int32, (m_rows, WIN), 1)
    left_of_seq = {w0: col < jnp.where(first_tile, LEFT_ROWS - w0, 0)
                   for w0 in range(0, LEFT_ROWS, GROUP * CHUNK)}

    def scores(g, p):
        w0 = g * GROUP * CHUNK
        ls = slice(p * V7X_LANES, (p + 1) * V7X_LANES)
        q2 = []
        for j in range(GROUP):
            qp = q_ref[w0 + j * CHUNK:w0 + (j + 1) * CHUNK, ls]
            zero = jnp.zeros_like(qp)
            q2 += [jnp.where(low_head, qp, zero), jnp.where(low_head, zero, qp)]
        s = lax.dot_general(jnp.concatenate(q2, axis=0), kwin[w0:w0 + WIN, ls],
                            (((1,), (1,)), ((), ())), preferred_element_type=F32)
        if w0 in left_of_seq:
            s = jnp.where(left_of_seq[w0], neg, s)
        return s

    def attend(g, p, s):
        w0 = g * GROUP * CHUNK
        ls = slice(p * V7X_LANES, (p + 1) * V7X_LANES)
        blocks = []
        for j in range(GROUP):
            t0 = (j // 2) * V7X_LANES
            for b0 in range(0, 2 * CHUNK, SOFTMAX_ROWS):
                r0 = 2 * j * CHUNK + b0
                sl = (s[r0:r0 + SOFTMAX_ROWS, t0:t0 + SLAB]
                      + bias_ref[j % 2, p, b0:b0 + SOFTMAX_ROWS, :])
                ex = jnp.exp2(sl - jnp.max(sl, axis=-1, keepdims=True)).astype(BF16)
                blocks.append(jnp.concatenate([ex, zcols] if t0 == 0 else [zcols, ex], axis=1))
        v_ones = jnp.concatenate([vwin[w0:w0 + WIN, ls], ones], axis=1)
        o2 = _dot(jnp.concatenate(blocks, axis=0), v_ones)
        o2 = o2[:, 0:V7X_LANES] / o2[:, V7X_LANES:2 * V7X_LANES]
        for j in range(GROUP):
            lo = o2[(2 * j) * CHUNK:(2 * j + 1) * CHUNK, :]
            hi = o2[(2 * j + 1) * CHUNK:(2 * j + 2) * CHUNK, :]
            obuf[w0 + j * CHUNK:w0 + (j + 1) * CHUNK, ls] = jnp.where(low_head, lo, hi).astype(BF16)

    bodies = [(g, p) for g in range(tm // (GROUP * CHUNK)) for p in range(d // V7X_LANES)]
    pending = {n: scores(*bodies[n]) for n in range(SCORES_AHEAD)}
    for n, body in enumerate(bodies):
        if n + SCORES_AHEAD < len(bodies):
            pending[n + SCORES_AHEAD] = scores(*bodies[n + SCORES_AHEAD])
        attend(*body, pending.pop(n))
    y = _dot(obuf[...], wo_ref[...])
    o_ref[...] = x_ref[...] + _gated_norm(y, ng_ref[1:2, :], mod_ref[2:3, :])


def _attn(x, q, k, v, bias, w_o, mod, norm_g, layer):
    bsz, s, d = x.shape
    tm = WIDE_ROW_TILE
    assert tm % (GROUP * CHUNK) == 0 and k.shape[1] == LEFT_ROWS + s
    n_pairs = d // V7X_LANES
    row_spec = pl.BlockSpec((None, tm, d), lambda b, i: (b, i, 0))
    win_spec = pl.BlockSpec((pl.Element(1), pl.Element(LEFT_ROWS + tm), pl.Element(d)),
                            lambda b, i: (b, i * tm, 0))
    return pl.pallas_call(
        _attn_kernel,
        grid=(bsz, s // tm),
        in_specs=[
            row_spec, row_spec, win_spec, win_spec,
            _resident((2, n_pairs, 2 * CHUNK, SLAB), lambda b, i: (0, 0, 0, 0)),
            _resident((d, d), lambda b, i: (0, 0)),
            pl.BlockSpec((None, None, 6, d), lambda b, i: (layer, b, 0, 0)),
            _resident((None, 4, d), lambda b, i: (layer, 0, 0)),
        ],
        out_specs=row_spec,
        out_shape=jax.ShapeDtypeStruct(x.shape, F32),
        scratch_shapes=[pltpu.VMEM((tm, d), BF16)],
        compiler_params=_params(2),
        name="band_attention",
    )(x, q, k, v, bias, w_o, mod, norm_g)


def kernel(x, c, mod_w, mod_b, norm_g, ffn_w_in, ffn_w_out, conv_w_in, conv_k, conv_w_out,
           kv_mod_w, kv_mod_b, kv_norm_g, w_kv, attn_w_q, attn_w_o, rel_bias):
    bsz, s, d = x.shape
    depth = mod_w.shape[0]
    n_a = conv_w_in.shape[0]
    n_heads = rel_bias.shape[1]
    assert d // n_heads == HEAD_DIM and s % WIDE_ROW_TILE == 0 and WIDE_ROW_TILE % ROW_TILE == 0

    mod = _modvec(c, mod_w, mod_b.reshape(depth, 1, 6 * d)).reshape(depth, bsz, 6, d)
    kvmod = _modvec(c, kv_mod_w[None], kv_mod_b.reshape(1, 1, 2 * d)).reshape(1, bsz, 2, d)

    conv_w_in, conv_w_out = conv_w_in.astype(BF16), conv_w_out.astype(BF16)
    kv_norm_g = kv_norm_g.reshape(1, d)
    w_kv = w_kv[None]

    k = v = None
    attn_w = None
    ffn_w = None
    for layer in range(depth):
        ffn_casts = [] if ffn_w else [(ffn_w_in, layer), (ffn_w_out, layer)]
        if layer < n_a:
            x, cast = _mixer(x, mod, norm_g, layer, conv_w_in, conv_k, conv_w_out, layer, ffn_casts)
        else:
            bi = layer - n_a
            if attn_w is None:
                attn_w = (w_kv[0].astype(BF16), attn_w_q[bi].astype(BF16), attn_w_o[bi].astype(BF16))
            w_kv_b, w_q_b, w_o_b = attn_w
            (k_new, v_new, q), cast = _proj(x, mod, kvmod, norm_g, layer, kv_norm_g,
                                            w_kv_b, w_q_b, ffn_casts)
            if layer == n_a:
                k, v = k_new, v_new
            bias = _bias_table(rel_bias, bi).reshape(2, n_heads // 2, 2 * CHUNK, SLAB)
            x = _attn(x, q, k, v, bias, w_o_b, mod, norm_g, layer)
        w_in, w_out = ffn_w or cast
        nxt = layer + 1
        next_casts = []
        if nxt < depth:
            next_casts = [(ffn_w_in, nxt), (ffn_w_out, nxt)]
            if nxt >= n_a:
                next_casts += [(w_kv, 0), (attn_w_q, nxt - n_a), (attn_w_o, nxt - n_a)]
        x, cast = _ffn(x, mod, norm_g, layer, w_in, w_out, next_casts)
        ffn_w = tuple(cast[0:2]) if nxt < depth else None
        attn_w = tuple(cast[2:5]) if len(cast) == 5 else None
    return x
```

```python
import functools
import math

import jax
import jax.numpy as jnp
from jax import lax
from jax.experimental import pallas as pl
from jax.experimental.pallas import tpu as pltpu

EPS = 1e-6
CHUNK = 64
N_LEFT_CHUNKS = 8
LEFT_ROWS = N_LEFT_CHUNKS * CHUNK
BAND = (N_LEFT_CHUNKS + 1) * CHUNK
SLAB = BAND + CHUNK
GROUP = 4
SCORES_AHEAD = 2
SOFTMAX_ROWS = 32
WIN = (GROUP + N_LEFT_CHUNKS) * CHUNK
HEAD_DIM = 64
MAX_REL = 2 * CHUNK
LOG2_E = math.log2(math.e)

V7X_LANES = 128
V7X_BF16_SUBLANES = 16
V7X_MXU_COLS = 256
ROW_TILE = 512
WIDE_ROW_TILE = 1024
MIXER_SUB_ROWS = 512
FFN_SUB_ROWS = 512
V7X_VMEM_LIMIT = 56 * 1024 * 1024

BF16 = jnp.bfloat16
F32 = jnp.float32


def _dot(a, b):
    return jnp.dot(a, b, preferred_element_type=F32)


def _unit_rms(x):
    return x * lax.rsqrt(jnp.mean(x * x, axis=-1, keepdims=True) + EPS)


def _norm_modulate(xn, g, scale, shift):
    return xn * (g * (1.0 + scale)) + shift


def _gated_norm(y, g, gate):
    return _unit_rms(y) * (gate * g)


def _resident(block_shape, index_map):
    return pl.BlockSpec(block_shape, index_map, pipeline_mode=pl.Buffered(1))


def _params(n_grid_dims):
    return pltpu.CompilerParams(
        dimension_semantics=("arbitrary",) * n_grid_dims,
        vmem_limit_bytes=V7X_VMEM_LIMIT)


def _with_casts(body, n_in, n_out, n_cast):
    def wrapped(*refs):
        srcs = refs[n_in:n_in + n_cast]
        outs_end = n_in + n_cast + n_out
        dsts = refs[outs_end:outs_end + n_cast]
        for src, dst in zip(srcs, dsts):
            dst[...] = src[...].astype(BF16)
        body(*refs[:n_in], *refs[n_in + n_cast:outs_end], *refs[outs_end + n_cast:])
    return wrapped


def _tiled_call(body, name, grid, in_specs, args, out_specs, out_shapes, scratch=(), casts=()):
    bsz, n_tiles = grid
    c_in, c_out, c_shapes, c_args = [], [], [], []
    for w, idx in casts:
        r, c = w.shape[1:]
        rows, cols = r // n_tiles, c // bsz
        assert rows * n_tiles == r and rows % V7X_BF16_SUBLANES == 0
        assert cols * bsz == c and cols % V7X_LANES == 0
        c_in.append(pl.BlockSpec((None, rows, cols), lambda b, i, idx=idx: (idx, i, b)))
        c_out.append(pl.BlockSpec((rows, cols), lambda b, i: (i, b)))
        c_shapes.append(jax.ShapeDtypeStruct((r, c), BF16))
        c_args.append(w)
    res = pl.pallas_call(
        _with_casts(body, len(in_specs), len(out_specs), len(casts)),
        grid=grid,
        in_specs=list(in_specs) + c_in,
        out_specs=list(out_specs) + c_out,
        out_shape=list(out_shapes) + c_shapes,
        scratch_shapes=list(scratch),
        compiler_params=_params(len(grid)),
        name=name,
    )(*args, *c_args)
    return res[:len(out_specs)], res[len(out_specs):]


def _modvec_kernel(c_ref, w_ref, b_ref, o_ref):
    s = jax.nn.silu(c_ref[...]).astype(BF16)
    o_ref[...] = _dot(s, w_ref[...].astype(BF16)) + b_ref[...]


def _modvec(c, w, b, col_tile=1024):
    n_layers, d, n = w.shape
    bsz = c.shape[0]
    return pl.pallas_call(
        _modvec_kernel,
        grid=(n_layers, n // col_tile),
        in_specs=[
            pl.BlockSpec((bsz, d), lambda l, j: (0, 0)),
            pl.BlockSpec((None, d, col_tile), lambda l, j: (l, 0, j)),
            pl.BlockSpec((None, 1, col_tile), lambda l, j: (l, 0, j)),
        ],
        out_specs=pl.BlockSpec((None, bsz, col_tile), lambda l, j: (l, 0, j)),
        out_shape=jax.ShapeDtypeStruct((n_layers, bsz, n), F32),
        compiler_params=_params(2),
        name="modvec",
    )(c, w, b)


def _mixer_kernel(x_ref, mod_ref, ng_ref, win_ref, ck_ref, wout_ref, o_ref, zbuf, *, n_sub):
    tm, d = x_ref.shape
    sh1, sc1, g1 = mod_ref[0:1, :], mod_ref[1:2, :], mod_ref[2:3, :]

    @pl.when(pl.program_id(1) == 0)
    def _():
        zbuf[0:8, :] = jnp.zeros((8, d), F32)

    rows = tm // n_sub
    for r0 in range(0, tm, rows):
        x = x_ref[r0:r0 + rows, :]
        h = _norm_modulate(_unit_rms(x), ng_ref[0:1, :], sc1, sh1).astype(BF16)
        us = []
        for c0 in range(0, d, V7X_MXU_COLS):
            cs = slice(c0, c0 + V7X_MXU_COLS)
            z = (_dot(h, win_ref[:, d + c0:d + c0 + V7X_MXU_COLS])
                 * _dot(h, win_ref[:, 2 * d + c0:2 * d + c0 + V7X_MXU_COLS]))
            zbuf[8 + r0:8 + r0 + rows, cs] = z
            conv = (ck_ref[0:1, cs] * zbuf[6 + r0:6 + r0 + rows, cs]
                    + ck_ref[1:2, cs] * zbuf[7 + r0:7 + r0 + rows, cs]
                    + ck_ref[2:3, cs] * z)
            us.append((_dot(h, win_ref[:, cs]) * conv).astype(BF16))
        y = _dot(jnp.concatenate(us, axis=1), wout_ref[...])
        o_ref[r0:r0 + rows, :] = x + _gated_norm(y, ng_ref[1:2, :], g1)
    zbuf[0:8, :] = zbuf[tm:tm + 8, :]


def _mixer(x, mod, norm_g, layer, w_in, conv_k, w_out, a_idx, casts):
    bsz, s, d = x.shape
    tm = WIDE_ROW_TILE
    row_spec = pl.BlockSpec((None, tm, d), lambda b, i: (b, i, 0))
    (out,), cast = _tiled_call(
        functools.partial(_mixer_kernel, n_sub=tm // MIXER_SUB_ROWS), "conv_mixer", (bsz, s // tm),
        in_specs=[
            row_spec,
            pl.BlockSpec((None, None, 6, d), lambda b, i: (layer, b, 0, 0)),
            _resident((None, 4, d), lambda b, i: (layer, 0, 0)),
            _resident((None, d, 3 * d), lambda b, i: (a_idx, 0, 0)),
            _resident((None, 3, d), lambda b, i: (a_idx, 0, 0)),
            _resident((None, d, d), lambda b, i: (a_idx, 0, 0)),
        ],
        args=(x, mod, norm_g, w_in, conv_k, w_out),
        out_specs=[row_spec],
        out_shapes=[jax.ShapeDtypeStruct(x.shape, F32)],
        scratch=[pltpu.VMEM((tm + 8, d), F32)],
        casts=casts)
    return out, cast


def _ffn_kernel(x_ref, mod_ref, ng_ref, win_ref, wout_ref, o_ref, *, n_sub):
    sh2, sc2, g2 = mod_ref[3:4, :], mod_ref[4:5, :], mod_ref[5:6, :]
    f = wout_ref.shape[0]
    rows = x_ref.shape[0] // n_sub
    for r in range(n_sub):
        rs = slice(r * rows, (r + 1) * rows)
        x = x_ref[rs, :]
        h = _norm_modulate(_unit_rms(x), ng_ref[2:3, :], sc2, sh2).astype(BF16)
        g = _dot(h, win_ref[:, 0:f])
        u = _dot(h, win_ref[:, f:2 * f])
        y = _dot((jax.nn.silu(g) * u).astype(BF16), wout_ref[...])
        o_ref[rs, :] = x + _gated_norm(y, ng_ref[3:4, :], g2)


def _ffn(x, mod, norm_g, layer, w_in, w_out, casts):
    bsz, s, d = x.shape
    f = w_out.shape[0]
    tm = WIDE_ROW_TILE
    row_spec = pl.BlockSpec((None, tm, d), lambda b, i: (b, i, 0))
    (out,), cast = _tiled_call(
        functools.partial(_ffn_kernel, n_sub=tm // FFN_SUB_ROWS), "swiglu_ffn", (bsz, s // tm),
        in_specs=[
            row_spec,
            pl.BlockSpec((None, None, 6, d), lambda b, i: (layer, b, 0, 0)),
            _resident((None, 4, d), lambda b, i: (layer, 0, 0)),
            _resident((d, 2 * f), lambda b, i: (0, 0)),
            _resident((f, d), lambda b, i: (0, 0)),
        ],
        args=(x, mod, norm_g, w_in, w_out),
        out_specs=[row_spec],
        out_shapes=[jax.ShapeDtypeStruct(x.shape, F32)],
        casts=casts)
    return out, cast


def _proj_kernel(x_ref, mod_ref, kvmod_ref, ng_ref, kvg_ref, wkv_ref, wq_ref,
                 k_ref, v_ref, q_ref):
    d = x_ref.shape[1]

    @pl.when(pl.program_id(1) == 0)
    def _():
        k_ref[...] = jnp.zeros(k_ref.shape, BF16)
        v_ref[...] = jnp.zeros(v_ref.shape, BF16)

    @pl.when(pl.program_id(1) > 0)
    def _():
        xn = _unit_rms(x_ref[...])
        hkv = _norm_modulate(xn, kvg_ref[...], kvmod_ref[1:2, :], kvmod_ref[0:1, :]).astype(BF16)
        k_ref[...] = _dot(hkv, wkv_ref[:, 0:d]).astype(BF16)
        v_ref[...] = _dot(hkv, wkv_ref[:, d:2 * d]).astype(BF16)
        h = _norm_modulate(xn, ng_ref[0:1, :], mod_ref[1:2, :], mod_ref[0:1, :]).astype(BF16)
        q_ref[...] = (_dot(h, wq_ref[...]) * (LOG2_E * HEAD_DIM ** -0.5)).astype(BF16)


def _proj(x, mod, kvmod, norm_g, layer, kv_norm_g, w_kv, w_q, casts):
    bsz, s, d = x.shape
    tm = ROW_TILE
    assert tm == LEFT_ROWS
    row_spec = pl.BlockSpec((None, tm, d), lambda b, i: (b, jnp.maximum(i - 1, 0), 0))
    kv_spec = pl.BlockSpec((None, tm, d), lambda b, i: (b, i, 0))
    kv = jax.ShapeDtypeStruct((bsz, LEFT_ROWS + s, d), BF16)
    return _tiled_call(
        _proj_kernel, "qkv_proj", (bsz, s // tm + 1),
        in_specs=[
            row_spec,
            pl.BlockSpec((None, None, 6, d), lambda b, i: (layer, b, 0, 0)),
            pl.BlockSpec((None, None, 2, d), lambda b, i: (0, b, 0, 0)),
            _resident((None, 4, d), lambda b, i: (layer, 0, 0)),
            _resident((1, d), lambda b, i: (0, 0)),
            _resident((d, 2 * d), lambda b, i: (0, 0)),
            _resident((d, d), lambda b, i: (0, 0)),
        ],
        args=(x, mod, kvmod, norm_g, kv_norm_g, w_kv, w_q),
        out_specs=[kv_spec, kv_spec, row_spec],
        out_shapes=[kv, kv, jax.ShapeDtypeStruct(x.shape, BF16)],
        casts=casts)


def _bias_kernel(rb_ref, o_ref):
    n_heads = rb_ref.shape[0]
    n_tab = 2 * MAX_REL
    rb = rb_ref[:, 0:n_tab]
    hi = rb.astype(BF16)
    r1 = rb - hi.astype(F32)
    mid = r1.astype(BF16)
    lo = (r1 - mid.astype(F32)).astype(BF16)
    pieces = jnp.concatenate([hi, mid, lo], axis=0)
    rb_top = rb_ref[:, n_tab:n_tab + 1]
    u = lax.broadcasted_iota(jnp.int32, (1, WIN), 1)
    m = jnp.where(u >= WIN - CHUNK, u - WIN, u)
    idx = jnp.clip(N_LEFT_CHUNKS * CHUNK - m, -MAX_REL, MAX_REL) + MAX_REL
    row = lax.broadcasted_iota(jnp.int32, (n_tab, WIN), 0)
    onehot = jnp.where(row == idx, 1.0, 0.0).astype(BF16)
    t3 = _dot(pieces, onehot)
    w = t3[0:n_heads] + t3[n_heads:2 * n_heads] + t3[2 * n_heads:3 * n_heads]
    w = (w + jnp.where(idx == n_tab, rb_top, 0.0)) * LOG2_E
    col = lax.broadcasted_iota(jnp.int32, (CHUNK, WIN), 1)
    neg = jnp.finfo(F32).min
    for var in range(2):
        kk = col - var * CHUNK
        in_band = (kk >= 0) & (kk < BAND)
        for h in range(n_heads):
            t = pltpu.roll(jnp.broadcast_to(w[h:h + 1, :], (CHUNK, WIN)), var * CHUNK, 1,
                           stride=1, stride_axis=0)
            o_ref[var, h] = jnp.where(in_band, t, neg)[:, 0:SLAB]


def _bias_table(rel_bias, b_idx):
    n_heads, n_rel = rel_bias.shape[1:]
    return pl.pallas_call(
        _bias_kernel,
        grid=(1,),
        in_specs=[pl.BlockSpec((None, n_heads, n_rel), lambda i: (b_idx, 0, 0))],
        out_specs=pl.BlockSpec((2, n_heads, CHUNK, SLAB), lambda i: (0, 0, 0, 0)),
        out_shape=jax.ShapeDtypeStruct((2, n_heads, CHUNK, SLAB), F32),
        compiler_params=_params(1),
        name="rel_bias_table",
    )(rel_bias)


def _attn_kernel(x_ref, q_ref, kwin, vwin, bias_ref, wo_ref, mod_ref, ng_ref, o_ref, obuf):
    kwin, vwin = kwin.at[0], vwin.at[0]
    tm, d = q_ref.shape
    m_rows = 2 * GROUP * CHUNK
    low_head = lax.broadcasted_iota(jnp.int32, (CHUNK, V7X_LANES), 1) < HEAD_DIM
    neg = jnp.finfo(F32).min
    ones = jnp.ones((WIN, V7X_LANES), BF16)
    zcols = jnp.zeros((SOFTMAX_ROWS, WIN - SLAB), BF16)
    first_tile = pl.program_id(1) == 0
    col = lax.broadcasted_iota(jnp.int32, (m_rows, WIN), 1)
    left_of_seq = {w0: col < jnp.where(first_tile, LEFT_ROWS - w0, 0)
                   for w0 in range(0, LEFT_ROWS, GROUP * CHUNK)}

    def scores(g, p):
        w0 = g * GROUP * CHUNK
        ls = slice(p * V7X_LANES, (p + 1) * V7X_LANES)
        q2 = []
        for j in range(GROUP):
            qp = q_ref[w0 + j * CHUNK:w0 + (j + 1) * CHUNK, ls]
            zero = jnp.zeros_like(qp)
            q2 += [jnp.where(low_head, qp, zero), jnp.where(low_head, zero, qp)]
        s = lax.dot_general(jnp.concatenate(q2, axis=0), kwin[w0:w0 + WIN, ls],
                            (((1,), (1,)), ((), ())), preferred_element_type=F32)
        if w0 in left_of_seq:
            s = jnp.where(left_of_seq[w0], neg, s)
        return s

    def attend(g, p, s):
        w0 = g * GROUP * CHUNK
        ls = slice(p * V7X_LANES, (p + 1) * V7X_LANES)
        blocks = []
        for j in range(GROUP):
            t0 = (j // 2) * V7X_LANES
            for b0 in range(0, 2 * CHUNK, SOFTMAX_ROWS):
                r0 = 2 * j * CHUNK + b0
                sl = (s[r0:r0 + SOFTMAX_ROWS, t0:t0 + SLAB]
                      + bias_ref[j % 2, p, b0:b0 + SOFTMAX_ROWS, :])
                ex = jnp.exp2(sl - jnp.max(sl, axis=-1, keepdims=True)).astype(BF16)
                blocks.append(jnp.concatenate([ex, zcols] if t0 == 0 else [zcols, ex], axis=1))
        v_ones = jnp.concatenate([vwin[w0:w0 + WIN, ls], ones], axis=1)
        o2 = _dot(jnp.concatenate(blocks, axis=0), v_ones)
        o2 = o2[:, 0:V7X_LANES] / o2[:, V7X_LANES:2 * V7X_LANES]
        for j in range(GROUP):
            lo = o2[(2 * j) * CHUNK:(2 * j + 1) * CHUNK, :]
            hi = o2[(2 * j + 1) * CHUNK:(2 * j + 2) * CHUNK, :]
            obuf[w0 + j * CHUNK:w0 + (j + 1) * CHUNK, ls] = jnp.where(low_head, lo, hi).astype(BF16)

    bodies = [(g, p) for g in range(tm // (GROUP * CHUNK)) for p in range(d // V7X_LANES)]
    pending = {n: scores(*bodies[n]) for n in range(SCORES_AHEAD)}
    for n, body in enumerate(bodies):
        if n + SCORES_AHEAD < len(bodies):
            pending[n + SCORES_AHEAD] = scores(*bodies[n + SCORES_AHEAD])
        attend(*body, pending.pop(n))
    y = _dot(obuf[...], wo_ref[...])
    o_ref[...] = x_ref[...] + _gated_norm(y, ng_ref[1:2, :], mod_ref[2:3, :])


def _attn(x, q, k, v, bias, w_o, mod, norm_g, layer):
    bsz, s, d = x.shape
    tm = WIDE_ROW_TILE
    assert tm % (GROUP * CHUNK) == 0 and k.shape[1] == LEFT_ROWS + s
    n_pairs = d // V7X_LANES
    row_spec = pl.BlockSpec((None, tm, d), lambda b, i: (b, i, 0))
    win_spec = pl.BlockSpec((pl.Element(1), pl.Element(LEFT_ROWS + tm), pl.Element(d)),
                            lambda b, i: (b, i * tm, 0))
    return pl.pallas_call(
        _attn_kernel,
        grid=(bsz, s // tm),
        in_specs=[
            row_spec, row_spec, win_spec, win_spec,
            _resident((2, n_pairs, 2 * CHUNK, SLAB), lambda b, i: (0, 0, 0, 0)),
            _resident((d, d), lambda b, i: (0, 0)),
            pl.BlockSpec((None, None, 6, d), lambda b, i: (layer, b, 0, 0)),
            _resident((None, 4, d), lambda b, i: (layer, 0, 0)),
        ],
        out_specs=row_spec,
        out_shape=jax.ShapeDtypeStruct(x.shape, F32),
        scratch_shapes=[pltpu.VMEM((tm, d), BF16)],
        compiler_params=_params(2),
        name="band_attention",
    )(x, q, k, v, bias, w_o, mod, norm_g)


def kernel(x, c, mod_w, mod_b, norm_g, ffn_w_in, ffn_w_out, conv_w_in, conv_k, conv_w_out,
           kv_mod_w, kv_mod_b, kv_norm_g, w_kv, attn_w_q, attn_w_o, rel_bias):
    bsz, s, d = x.shape
    depth = mod_w.shape[0]
    n_a = conv_w_in.shape[0]
    n_heads = rel_bias.shape[1]
    assert d // n_heads == HEAD_DIM and s % WIDE_ROW_TILE == 0 and WIDE_ROW_TILE % ROW_TILE == 0

    mod = _modvec(c, mod_w, mod_b.reshape(depth, 1, 6 * d)).reshape(depth, bsz, 6, d)
    kvmod = _modvec(c, kv_mod_w[None], kv_mod_b.reshape(1, 1, 2 * d)).reshape(1, bsz, 2, d)

    conv_w_in, conv_w_out = conv_w_in.astype(BF16), conv_w_out.astype(BF16)
    kv_norm_g = kv_norm_g.reshape(1, d)
    w_kv = w_kv[None]

    k = v = None
    attn_w = None
    ffn_w = None
    for layer in range(depth):
        ffn_casts = [] if ffn_w else [(ffn_w_in, layer), (ffn_w_out, layer)]
        if layer < n_a:
            x, cast = _mixer(x, mod, norm_g, layer, conv_w_in, conv_k, conv_w_out, layer, ffn_casts)
        else:
            bi = layer - n_a
            if attn_w is None:
                attn_w = (w_kv[0].astype(BF16), attn_w_q[bi].astype(BF16), attn_w_o[bi].astype(BF16))
            w_kv_b, w_q_b, w_o_b = attn_w
            (k_new, v_new, q), cast = _proj(x, mod, kvmod, norm_g, layer, kv_norm_g,
                                            w_kv_b, w_q_b, ffn_casts)
            if layer == n_a:
                k, v = k_new, v_new
            bias = _bias_table(rel_bias, bi).reshape(2, n_heads // 2, 2 * CHUNK, SLAB)
            x = _attn(x, q, k, v, bias, w_o_b, mod, norm_g, layer)
        w_in, w_out = ffn_w or cast
        nxt = layer + 1
        next_casts = []
        if nxt < depth:
            next_casts = [(ffn_w_in, nxt), (ffn_w_out, nxt)]
            if nxt >= n_a:
                next_casts += [(w_kv, 0), (attn_w_q, nxt - n_a), (attn_w_o, nxt - n_a)]
        x, cast = _ffn(x, mod, norm_g, layer, w_in, w_out, next_casts)
        ffn_w = tuple(cast[0:2]) if nxt < depth else None
        attn_w = tuple(cast[2:5]) if len(cast) == 5 else None
    return x
```

```python
import functools
import math

import jax
import jax.numpy as jnp
from jax import lax
from jax.experimental import pallas as pl
from jax.experimental.pallas import tpu as pltpu

EPS = 1e-6
CHUNK = 64
N_LEFT_CHUNKS = 8
LEFT_ROWS = N_LEFT_CHUNKS * CHUNK
BAND = (N_LEFT_CHUNKS + 1) * CHUNK
SLAB = BAND + CHUNK
GROUP = 4
SCORES_AHEAD = 2
SOFTMAX_ROWS = 32
WIN = (GROUP + N_LEFT_CHUNKS) * CHUNK
HEAD_DIM = 64
MAX_REL = 2 * CHUNK
LOG2_E = math.log2(math.e)

V7X_LANES = 128
V7X_BF16_SUBLANES = 16
V7X_MXU_COLS = 256
ROW_TILE = 512
WIDE_ROW_TILE = 1024
MIXER_SUB_ROWS = 512
FFN_SUB_ROWS = 256
V7X_VMEM_LIMIT = 56 * 1024 * 1024

BF16 = jnp.bfloat16
F32 = jnp.float32


def _dot(a, b):
    return jnp.dot(a, b, preferred_element_type=F32)


def _unit_rms(x):
    return x * lax.rsqrt(jnp.mean(x * x, axis=-1, keepdims=True) + EPS)


def _norm_modulate(xn, g, scale, shift):
    return xn * (g * (1.0 + scale)) + shift


def _gated_norm(y, g, gate):
    return _unit_rms(y) * (gate * g)


def _resident(block_shape, index_map):
    return pl.BlockSpec(block_shape, index_map, pipeline_mode=pl.Buffered(1))


def _params(n_grid_dims):
    return pltpu.CompilerParams(
        dimension_semantics=("arbitrary",) * n_grid_dims,
        vmem_limit_bytes=V7X_VMEM_LIMIT)


def _with_casts(body, n_in, n_out, n_cast):
    def wrapped(*refs):
        srcs = refs[n_in:n_in + n_cast]
        outs_end = n_in + n_cast + n_out
        dsts = refs[outs_end:outs_end + n_cast]
        for src, dst in zip(srcs, dsts):
            dst[...] = src[...].astype(BF16)
        body(*refs[:n_in], *refs[n_in + n_cast:outs_end], *refs[outs_end + n_cast:])
    return wrapped


def _tiled_call(body, name, grid, in_specs, args, out_specs, out_shapes, scratch=(), casts=()):
    bsz, n_tiles = grid
    c_in, c_out, c_shapes, c_args = [], [], [], []
    for w, idx in casts:
        r, c = w.shape[1:]
        rows, cols = r // n_tiles, c // bsz
        assert rows * n_tiles == r and rows % V7X_BF16_SUBLANES == 0
        assert cols * bsz == c and cols % V7X_LANES == 0
        c_in.append(pl.BlockSpec((None, rows, cols), lambda b, i, idx=idx: (idx, i, b)))
        c_out.append(pl.BlockSpec((rows, cols), lambda b, i: (i, b)))
        c_shapes.append(jax.ShapeDtypeStruct((r, c), BF16))
        c_args.append(w)
    res = pl.pallas_call(
        _with_casts(body, len(in_specs), len(out_specs), len(casts)),
        grid=grid,
        in_specs=list(in_specs) + c_in,
        out_specs=list(out_specs) + c_out,
        out_shape=list(out_shapes) + c_shapes,
        scratch_shapes=list(scratch),
        compiler_params=_params(len(grid)),
        name=name,
    )(*args, *c_args)
    return res[:len(out_specs)], res[len(out_specs):]


def _modvec_kernel(c_ref, w_ref, b_ref, o_ref):
    s = jax.nn.silu(c_ref[...]).astype(BF16)
    o_ref[...] = _dot(s, w_ref[...].astype(BF16)) + b_ref[...]


def _modvec(c, w, b, col_tile=1024):
    n_layers, d, n = w.shape
    bsz = c.shape[0]
    return pl.pallas_call(
        _modvec_kernel,
        grid=(n_layers, n // col_tile),
        in_specs=[
            pl.BlockSpec((bsz, d), lambda l, j: (0, 0)),
            pl.BlockSpec((None, d, col_tile), lambda l, j: (l, 0, j)),
            pl.BlockSpec((None, 1, col_tile), lambda l, j: (l, 0, j)),
        ],
        out_specs=pl.BlockSpec((None, bsz, col_tile), lambda l, j: (l, 0, j)),
        out_shape=jax.ShapeDtypeStruct((n_layers, bsz, n), F32),
        compiler_params=_params(2),
        name="modvec",
    )(c, w, b)


def _mixer_kernel(x_ref, mod_ref, ng_ref, win32_ref, ck_ref, wout32_ref, o_ref, zbuf,
                  win_ref, wout_ref, *, n_sub):
    tm, d = x_ref.shape
    sh1, sc1, g1 = mod_ref[0:1, :], mod_ref[1:2, :], mod_ref[2:3, :]

    @pl.when((pl.program_id(0) == 0) & (pl.program_id(1) == 0))
    def _():
        win_ref[...] = win32_ref[...].astype(BF16)
        wout_ref[...] = wout32_ref[...].astype(BF16)

    @pl.when(pl.program_id(1) == 0)
    def _():
        zbuf[0:8, :] = jnp.zeros((8, d), F32)

    rows = tm // n_sub
    for r0 in range(0, tm, rows):
        x = x_ref[r0:r0 + rows, :]
        h = _norm_modulate(_unit_rms(x), ng_ref[0:1, :], sc1, sh1).astype(BF16)
        us = []
        for c0 in range(0, d, V7X_MXU_COLS):
            cs = slice(c0, c0 + V7X_MXU_COLS)
            z = (_dot(h, win_ref[:, d + c0:d + c0 + V7X_MXU_COLS])
                 * _dot(h, win_ref[:, 2 * d + c0:2 * d + c0 + V7X_MXU_COLS]))
            zbuf[8 + r0:8 + r0 + rows, cs] = z
            conv = (ck_ref[0:1, cs] * zbuf[6 + r0:6 + r0 + rows, cs]
                    + ck_ref[1:2, cs] * zbuf[7 + r0:7 + r0 + rows, cs]
                    + ck_ref[2:3, cs] * z)
            us.append((_dot(h, win_ref[:, cs]) * conv).astype(BF16))
        y = _dot(jnp.concatenate(us, axis=1), wout_ref[...])
        o_ref[r0:r0 + rows, :] = x + _gated_norm(y, ng_ref[1:2, :], g1)
    zbuf[0:8, :] = zbuf[tm:tm + 8, :]


def _mixer(x, mod, norm_g, layer, w_in, conv_k, w_out, a_idx, casts):
    bsz, s, d = x.shape
    tm = WIDE_ROW_TILE
    row_spec = pl.BlockSpec((None, tm, d), lambda b, i: (b, i, 0))
    (out,), cast = _tiled_call(
        functools.partial(_mixer_kernel, n_sub=tm // MIXER_SUB_ROWS), "conv_mixer", (bsz, s // tm),
        in_specs=[
            row_spec,
            pl.BlockSpec((None, None, 6, d), lambda b, i: (layer, b, 0, 0)),
            _resident((None, 4, d), lambda b, i: (layer, 0, 0)),
            _resident((None, d, 3 * d), lambda b, i: (a_idx, 0, 0)),
            _resident((None, 3, d), lambda b, i: (a_idx, 0, 0)),
            _resident((None, d, d), lambda b, i: (a_idx, 0, 0)),
        ],
        args=(x, mod, norm_g, w_in, conv_k, w_out),
        out_specs=[row_spec],
        out_shapes=[jax.ShapeDtypeStruct(x.shape, F32)],
        scratch=[pltpu.VMEM((tm + 8, d), F32), pltpu.VMEM((d, 3 * d), BF16), pltpu.VMEM((d, d), BF16)],
        casts=casts)
    return out, cast


def _ffn_kernel(x_ref, mod_ref, ng_ref, win_ref, wout_ref, o_ref, *, n_sub):
    sh2, sc2, g2 = mod_ref[3:4, :], mod_ref[4:5, :], mod_ref[5:6, :]
    f = wout_ref.shape[0]
    rows = x_ref.shape[0] // n_sub
    for r in range(n_sub):
        rs = slice(r * rows, (r + 1) * rows)
        x = x_ref[rs, :]
        h = _norm_modulate(_unit_rms(x), ng_ref[2:3, :], sc2, sh2).astype(BF16)
        g = _dot(h, win_ref[:, 0:f])
        u = _dot(h, win_ref[:, f:2 * f])
        y = _dot((jax.nn.silu(g) * u).astype(BF16), wout_ref[...])
        o_ref[rs, :] = x + _gated_norm(y, ng_ref[3:4, :], g2)


def _ffn(x, mod, norm_g, layer, w_in, w_out, casts):
    bsz, s, d = x.shape
    f = w_out.shape[0]
    tm = WIDE_ROW_TILE
    row_spec = pl.BlockSpec((None, tm, d), lambda b, i: (b, i, 0))
    (out,), cast = _tiled_call(
        functools.partial(_ffn_kernel, n_sub=tm // FFN_SUB_ROWS), "swiglu_ffn", (bsz, s // tm),
        in_specs=[
            row_spec,
            pl.BlockSpec((None, None, 6, d), lambda b, i: (layer, b, 0, 0)),
            _resident((None, 4, d), lambda b, i: (layer, 0, 0)),
            _resident((d, 2 * f), lambda b, i: (0, 0)),
            _resident((f, d), lambda b, i: (0, 0)),
        ],
        args=(x, mod, norm_g, w_in, w_out),
        out_specs=[row_spec],
        out_shapes=[jax.ShapeDtypeStruct(x.shape, F32)],
        casts=casts)
    return out, cast


def _proj_kernel(x_ref, mod_ref, kvmod_ref, ng_ref, kvg_ref, wkv_ref, wq_ref,
                 k_ref, v_ref, q_ref):
    d = x_ref.shape[1]

    @pl.when(pl.program_id(1) == 0)
    def _():
        k_ref[...] = jnp.zeros(k_ref.shape, BF16)
        v_ref[...] = jnp.zeros(v_ref.shape, BF16)

    @pl.when(pl.program_id(1) > 0)
    def _():
        xn = _unit_rms(x_ref[...])
        hkv = _norm_modulate(xn, kvg_ref[...], kvmod_ref[1:2, :], kvmod_ref[0:1, :]).astype(BF16)
        k_ref[...] = _dot(hkv, wkv_ref[:, 0:d]).astype(BF16)
        v_ref[...] = _dot(hkv, wkv_ref[:, d:2 * d]).astype(BF16)
        h = _norm_modulate(xn, ng_ref[0:1, :], mod_ref[1:2, :], mod_ref[0:1, :]).astype(BF16)
        q_ref[...] = (_dot(h, wq_ref[...]) * (LOG2_E * HEAD_DIM ** -0.5)).astype(BF16)


def _proj(x, mod, kvmod, norm_g, layer, kv_norm_g, w_kv, w_q, casts):
    bsz, s, d = x.shape
    tm = ROW_TILE
    assert tm == LEFT_ROWS
    row_spec = pl.BlockSpec((None, tm, d), lambda b, i: (b, jnp.maximum(i - 1, 0), 0))
    kv_spec = pl.BlockSpec((None, tm, d), lambda b, i: (b, i, 0))
    kv = jax.ShapeDtypeStruct((bsz, LEFT_ROWS + s, d), BF16)
    return _tiled_call(
        _proj_kernel, "qkv_proj", (bsz, s // tm + 1),
        in_specs=[
            row_spec,
            pl.BlockSpec((None, None, 6, d), lambda b, i: (layer, b, 0, 0)),
            pl.BlockSpec((None, None, 2, d), lambda b, i: (0, b, 0, 0)),
            _resident((None, 4, d), lambda b, i: (layer, 0, 0)),
            _resident((1, d), lambda b, i: (0, 0)),
            _resident((d, 2 * d), lambda b, i: (0, 0)),
            _resident((d, d), lambda b, i: (0, 0)),
        ],
        args=(x, mod, kvmod, norm_g, kv_norm_g, w_kv, w_q),
        out_specs=[kv_spec, kv_spec, row_spec],
        out_shapes=[kv, kv, jax.ShapeDtypeStruct(x.shape, BF16)],
        casts=casts)


def _bias_kernel(rb_ref, o_ref):
    n_heads = rb_ref.shape[0]
    n_tab = 2 * MAX_REL
    rb = rb_ref[:, 0:n_tab]
    hi = rb.astype(BF16)
    r1 = rb - hi.astype(F32)
    mid = r1.astype(BF16)
    lo = (r1 - mid.astype(F32)).astype(BF16)
    pieces = jnp.concatenate([hi, mid, lo], axis=0)
    rb_top = rb_ref[:, n_tab:n_tab + 1]
    u = lax.broadcasted_iota(jnp.int32, (1, WIN), 1)
    m = jnp.where(u >= WIN - CHUNK, u - WIN, u)
    idx = jnp.clip(N_LEFT_CHUNKS * CHUNK - m, -MAX_REL, MAX_REL) + MAX_REL
    row = lax.broadcasted_iota(jnp.int32, (n_tab, WIN), 0)
    onehot = jnp.where(row == idx, 1.0, 0.0).astype(BF16)
    t3 = _dot(pieces, onehot)
    w = t3[0:n_heads] + t3[n_heads:2 * n_heads] + t3[2 * n_heads:3 * n_heads]
    w = (w + jnp.where(idx == n_tab, rb_top, 0.0)) * LOG2_E
    col = lax.broadcasted_iota(jnp.int32, (CHUNK, WIN), 1)
    neg = jnp.finfo(F32).min
    for var in range(2):
        kk = col - var * CHUNK
        in_band = (kk >= 0) & (kk < BAND)
        for h in range(n_heads):
            t = pltpu.roll(jnp.broadcast_to(w[h:h + 1, :], (CHUNK, WIN)), var * CHUNK, 1,
                           stride=1, stride_axis=0)
            o_ref[var, h] = jnp.where(in_band, t, neg)[:, 0:SLAB]


def _bias_table(rel_bias, b_idx):
    n_heads, n_rel = rel_bias.shape[1:]
    return pl.pallas_call(
        _bias_kernel,
        grid=(1,),
        in_specs=[pl.BlockSpec((None, n_heads, n_rel), lambda i: (b_idx, 0, 0))],
        out_specs=pl.BlockSpec((2, n_heads, CHUNK, SLAB), lambda i: (0, 0, 0, 0)),
        out_shape=jax.ShapeDtypeStruct((2, n_heads, CHUNK, SLAB), F32),
        compiler_params=_params(1),
        name="rel_bias_table",
    )(rel_bias)


def _attn_kernel(x_ref, q_ref, kwin, vwin, bias_ref, wo_ref, mod_ref, ng_ref, o_ref, obuf):
    kwin, vwin = kwin.at[0], vwin.at[0]
    tm, d = q_ref.shape
    m_rows = 2 * GROUP * CHUNK
    low_head = lax.broadcasted_iota(jnp.int32, (CHUNK, V7X_LANES), 1) < HEAD_DIM
    neg = jnp.finfo(F32).min
    ones = jnp.ones((WIN, V7X_LANES), BF16)
    zcols = jnp.zeros((SOFTMAX_ROWS, WIN - SLAB), BF16)
    first_tile = pl.program_id(1) == 0
    col = lax.broadcasted_iota(jnp.int32, (m_rows, WIN), 1)
    left_of_seq = {w0: col < jnp.where(first_tile, LEFT_ROWS - w0, 0)
                   for w0 in range(0, LEFT_ROWS, GROUP * CHUNK)}

    def scores(g, p):
        w0 = g * GROUP * CHUNK
        ls = slice(p * V7X_LANES, (p + 1) * V7X_LANES)
        q2 = []
        for j in range(GROUP):
            qp = q_ref[w0 + j * CHUNK:w0 + (j + 1) * CHUNK, ls]
            zero = jnp.zeros_like(qp)
            q2 += [jnp.where(low_head, qp, zero), jnp.where(low_head, zero, qp)]
        s = lax.dot_general(jnp.concatenate(q2, axis=0), kwin[w0:w0 + WIN, ls],
                            (((1,), (1,)), ((), ())), preferred_element_type=F32)
        if w0 in left_of_seq:
            s = jnp.where(left_of_seq[w0], neg, s)
        return s

    def attend(g, p, s):
        w0 = g * GROUP * CHUNK
        ls = slice(p * V7X_LANES, (p + 1) * V7X_LANES)
        blocks = []
        for j in range(GROUP):
            t0 = (j // 2) * V7X_LANES
            for b0 in range(0, 2 * CHUNK, SOFTMAX_ROWS):
                r0 = 2 * j * CHUNK + b0
                sl = (s[r0:r0 + SOFTMAX_ROWS, t0:t0 + SLAB]
                      + bias_ref[j % 2, p, b0:b0 + SOFTMAX_ROWS, :])
                ex = jnp.exp2(sl - jnp.max(sl, axis=-1, keepdims=True)).astype(BF16)
                blocks.append(jnp.concatenate([ex, zcols] if t0 == 0 else [zcols, ex], axis=1))
        v_ones = jnp.concatenate([vwin[w0:w0 + WIN, ls], ones], axis=1)
        o2 = _dot(jnp.concatenate(blocks, axis=0), v_ones)
        o2 = o2[:, 0:V7X_LANES] / o2[:, V7X_LANES:2 * V7X_LANES]
        for j in range(GROUP):
            lo = o2[(2 * j) * CHUNK:(2 * j + 1) * CHUNK, :]
            hi = o2[(2 * j + 1) * CHUNK:(2 * j + 2) * CHUNK, :]
            obuf[w0 + j * CHUNK:w0 + (j + 1) * CHUNK, ls] = jnp.where(low_head, lo, hi).astype(BF16)

    bodies = [(g, p) for g in range(tm // (GROUP * CHUNK)) for p in range(d // V7X_LANES)]
    pending = {n: scores(*bodies[n]) for n in range(SCORES_AHEAD)}
    for n, body in enumerate(bodies):
        if n + SCORES_AHEAD < len(bodies):
            pending[n + SCORES_AHEAD] = scores(*bodies[n + SCORES_AHEAD])
        attend(*body, pending.pop(n))
    y = _dot(obuf[...], wo_ref[...])
    o_ref[...] = x_ref[...] + _gated_norm(y, ng_ref[1:2, :], mod_ref[2:3, :])


def _attn(x, q, k, v, bias, w_o, mod, norm_g, layer):
    bsz, s, d = x.shape
    tm = WIDE_ROW_TILE
    assert tm % (GROUP * CHUNK) == 0 and k.shape[1] == LEFT_ROWS + s
    n_pairs = d // V7X_LANES
    row_spec = pl.BlockSpec((None, tm, d), lambda b, i: (b, i, 0))
    win_spec = pl.BlockSpec((pl.Element(1), pl.Element(LEFT_ROWS + tm), pl.Element(d)),
                            lambda b, i: (b, i * tm, 0))
    return pl.pallas_call(
        _attn_kernel,
        grid=(bsz, s // tm),
        in_specs=[
            row_spec, row_spec, win_spec, win_spec,
            _resident((2, n_pairs, 2 * CHUNK, SLAB), lambda b, i: (0, 0, 0, 0)),
            _resident((d, d), lambda b, i: (0, 0)),
            pl.BlockSpec((None, None, 6, d), lambda b, i: (layer, b, 0, 0)),
            _resident((None, 4, d), lambda b, i: (layer, 0, 0)),
        ],
        out_specs=row_spec,
        out_shape=jax.ShapeDtypeStruct(x.shape, F32),
        scratch_shapes=[pltpu.VMEM((tm, d), BF16)],
        compiler_params=_params(2),
        name="band_attention",
    )(x, q, k, v, bias, w_o, mod, norm_g)


def kernel(x, c, mod_w, mod_b, norm_g, ffn_w_in, ffn_w_out, conv_w_in, conv_k, conv_w_out,
           kv_mod_w, kv_mod_b, kv_norm_g, w_kv, attn_w_q, attn_w_o, rel_bias):
    bsz, s, d = x.shape
    depth = mod_w.shape[0]
    n_a = conv_w_in.shape[0]
    n_heads = rel_bias.shape[1]
    assert d // n_heads == HEAD_DIM and s % WIDE_ROW_TILE == 0 and WIDE_ROW_TILE % ROW_TILE == 0

    mod = _modvec(c, mod_w, mod_b.reshape(depth, 1, 6 * d)).reshape(depth, bsz, 6, d)
    kvmod = _modvec(c, kv_mod_w[None], kv_mod_b.reshape(1, 1, 2 * d)).reshape(1, bsz, 2, d)

    kv_norm_g = kv_norm_g.reshape(1, d)
    w_kv = w_kv[None]

    k = v = None
    attn_w = None
    ffn_w = None
    for layer in range(depth):
        ffn_casts = [] if ffn_w else [(ffn_w_in, layer), (ffn_w_out, layer)]
        if layer < n_a:
            x, cast = _mixer(x, mod, norm_g, layer, conv_w_in, conv_k, conv_w_out, layer, ffn_casts)
        else:
            bi = layer - n_a
            if attn_w is None:
                attn_w = (w_kv[0].astype(BF16), attn_w_q[bi].astype(BF16), attn_w_o[bi].astype(BF16))
            w_kv_b, w_q_b, w_o_b = attn_w
            (k_new, v_new, q), cast = _proj(x, mod, kvmod, norm_g, layer, kv_norm_g,
                                            w_kv_b, w_q_b, ffn_casts)
            if layer == n_a:
                k, v = k_new, v_new
            bias = _bias_table(rel_bias, bi).reshape(2, n_heads // 2, 2 * CHUNK, SLAB)
            x = _attn(x, q, k, v, bias, w_o_b, mod, norm_g, layer)
        w_in, w_out = ffn_w or cast
        nxt = layer + 1
        next_casts = []
        if nxt < depth:
            next_casts = [(ffn_w_in, nxt), (ffn_w_out, nxt)]
            if nxt >= n_a:
                next_casts += [(w_kv, 0), (attn_w_q, nxt - n_a), (attn_w_o, nxt - n_a)]
        x, cast = _ffn(x, mod, norm_g, layer, w_in, w_out, next_casts)
        ffn_w = tuple(cast[0:2]) if nxt < depth else None
        attn_w = tuple(cast[2:5]) if len(cast) == 5 else None
    return x
```
